```python
import jax, jax.numpy as jnp
from jax import lax
import numpy as np

D_MODEL = 1024
BATCH = 8
SEQ = 4096
DEPTH = 4

N_MIXERS = 2
N_MEM = 256
GRID_W = 64
TOKEN_W = 3 * D_MODEL // 4
MEM_W = D_MODEL // 4
MIX_W = TOKEN_W + MEM_W
HEAD_DIM = 64
NA_HEADS = TOKEN_W // HEAD_DIM
WIN_H = 8
WIN_W = 16
ML_HEADS = 4
ML_DH = TOKEN_W // ML_HEADS
ML_CHUNK = 64
CONV_K = 5
ML_PROJ = 4 * TOKEN_W + 4 * ML_HEADS
MEM_HEADS = 4
MEM_DH = MEM_W // MEM_HEADS
N_GROUPS = 4
EXPERTS_PER_GROUP = 8
N_EXPERTS = N_GROUPS * EXPERTS_PER_GROUP
TOP_K = 2
D_EXPERT = D_MODEL // 2
MOE_BLOCK = 128
LN_EPS = 1e-5
ALPHA = (2 * DEPTH) ** 0.25
BETA = (8 * DEPTH) ** -0.25
N_NA_LAYERS = (DEPTH + 1) // 2
N_ML_LAYERS = DEPTH // 2
NA_IN = 3 * TOKEN_W + MEM_W
ML_IN = ML_PROJ + MEM_W

kernel_name = "hybrid_natten_mlstm_hmoe_encoder"


def layer_norm(x, g, b):
    xf = x.astype(jnp.float32)
    mu = xf.mean(-1, keepdims=True)
    var = jnp.square(xf - mu).mean(-1, keepdims=True)
    y = (xf - mu) * lax.rsqrt(var + LN_EPS)
    return (y * g + b).astype(x.dtype)


def neighbourhood_attention(q, k, v, rpb):
    B, T, H, Dh = q.shape
    rows = T // GRID_W
    kh = min(WIN_H, rows)
    to_grid = lambda a: a.reshape(B, rows, GRID_W, H, Dh).transpose(1, 0, 3, 2, 4)
    qg, kg, vg = to_grid(q), to_grid(k), to_grid(v)
    cols = jnp.arange(GRID_W)
    col_start = jnp.clip(cols - WIN_W // 2, 0, GRID_W - WIN_W)
    col_idx = col_start[:, None] + jnp.arange(WIN_W)
    col_bias_idx = col_idx - cols[:, None] + (WIN_W - 1)
    scale = Dh ** -0.5

    def row_block(r):
        row_start = jnp.clip(r - kh // 2, 0, rows - kh)
        k_band = lax.dynamic_slice_in_dim(kg, row_start, kh, axis=0)
        v_band = lax.dynamic_slice_in_dim(vg, row_start, kh, axis=0)
        k_win = k_band[:, :, :, col_idx]
        v_win = v_band[:, :, :, col_idx]
        q_row = lax.dynamic_index_in_dim(qg, r, axis=0, keepdims=False)
        row_bias_idx = row_start + jnp.arange(kh) - r + (WIN_H - 1)
        bias = rpb[:, row_bias_idx[None, :, None], col_bias_idx[:, None, :]]
        s = jnp.einsum('bhqd,rbhqkd->bhqrk', q_row, k_win).astype(jnp.float32) * scale + bias
        p = jax.nn.softmax(s.reshape(B, H, GRID_W, kh * WIN_W), axis=-1)
        p = p.reshape(B, H, GRID_W, kh, WIN_W).astype(v.dtype)
        return jnp.einsum('bhqrk,rbhqkd->bhqd', p, v_win)

    out = lax.map(row_block, jnp.arange(rows))
    return out.transpose(1, 0, 3, 2, 4).reshape(B, T, H * Dh)


def mlstm_scan(q, k, v, log_i, log_f):
    B, H, T, Dh = q.shape
    L = min(ML_CHUNK, T)
    nc = T // L
    to_chunks = lambda a: jnp.moveaxis(a.reshape(B, H, nc, L, *a.shape[3:]), 2, 0)
    tril = jnp.tril(jnp.ones((L, L), dtype=bool))

    def step(carry, inp):
        C, n, m = carry
        qc, kc, vc, lic, lfc = inp
        b = jnp.cumsum(lfc, axis=-1)
        D = b[..., :, None] - b[..., None, :] + lic[..., None, :]
        D = jnp.where(tril, D, -jnp.inf)
        inter = b + m[..., None]
        m_t = jnp.maximum(inter, D.max(-1))
        w_inter = jnp.exp(inter - m_t)
        S = jnp.einsum('bhld,bhsd->bhls', qc, kc) * jnp.exp(D - m_t[..., None])
        num = w_inter[..., None] * jnp.einsum('bhld,bhde->bhle', qc, C) + jnp.einsum('bhls,bhse->bhle', S, vc)
        den = w_inter * jnp.einsum('bhld,bhd->bhl', qc, n) + S.sum(-1)
        h = num / jnp.maximum(jnp.abs(den), jnp.exp(-m_t))[..., None]
        b_last = b[..., -1]
        m_new = m_t[..., -1]
        decay = jnp.exp(b_last + m - m_new)
        w_src = jnp.exp(b_last[..., None] - b + lic - m_new[..., None])
        C_new = decay[..., None, None] * C + jnp.einsum('bhs,bhsd,bhse->bhde', w_src, kc, vc)
        n_new = decay[..., None] * n + jnp.einsum('bhs,bhsd->bhd', w_src, kc)
        return (C_new, n_new, m_new), h

    init = (jnp.zeros((B, H, Dh, Dh), jnp.float32), jnp.zeros((B, H, Dh), jnp.float32),
            jnp.zeros((B, H), jnp.float32))
    _, hs = lax.scan(step, init, (to_chunks(q), to_chunks(k), to_chunks(v), to_chunks(log_i), to_chunks(log_f)))
    return jnp.moveaxis(hs, 0, 2).reshape(B, H, T, Dh)


def centred_dwconv(x, w, b):
    C = x.shape[-1]
    y = lax.conv_general_dilated(x, w[:, None, :], window_strides=(1,),
                                 padding=[(CONV_K // 2, CONV_K // 2)],
                                 dimension_numbers=('NWC', 'WIO', 'NWC'),
                                 feature_group_count=C)
    return y + b


def mlstm_mixer(proj, conv_w, conv_b, gate_b, norm_g):
    B, T, _ = proj.shape
    qk_pre = proj[..., :2 * TOKEN_W]
    v = proj[..., 2 * TOKEN_W:3 * TOKEN_W]
    o_pre = proj[..., 3 * TOKEN_W:4 * TOKEN_W]
    gates = proj[..., 4 * TOKEN_W:]
    qk = jax.nn.silu(centred_dwconv(qk_pre, conv_w, conv_b))
    q, k = qk[..., :TOKEN_W], qk[..., TOKEN_W:]
    heads = lambda a: a.reshape(B, T, ML_HEADS, ML_DH).transpose(0, 2, 1, 3).astype(jnp.float32)
    qh, kh, vh = heads(q) * (ML_DH ** -0.5), heads(k), heads(v)
    g = (gates.reshape(B, T, 4, ML_HEADS).astype(jnp.float32) + gate_b).transpose(2, 0, 3, 1)
    log_i = g[:2]
    log_f = jax.nn.log_sigmoid(g[2:])
    flip = lambda a: jnp.flip(a, axis=2)
    h_fwd = mlstm_scan(qh, kh, vh, log_i[0], log_f[0])
    h_bwd = flip(mlstm_scan(flip(qh), flip(kh), flip(vh), flip(log_i[1]), flip(log_f[1])))
    h = (h_fwd + h_bwd).transpose(0, 2, 1, 3)
    mu = h.mean(-1, keepdims=True)
    var = jnp.square(h - mu).mean(-1, keepdims=True)
    h = ((h - mu) * lax.rsqrt(var + LN_EPS)).reshape(B, T, TOKEN_W) * norm_g
    return (jax.nn.sigmoid(o_pre.astype(jnp.float32)) * h).astype(proj.dtype)


def memory_attention(q, mem_k, mem_v):
    B, T, _ = q.shape
    qh = q.reshape(B, T, MEM_HEADS, MEM_DH)
    s = jnp.einsum('bthd,bmhd->bhtm', qh, mem_k).astype(jnp.float32) * (MEM_DH ** -0.5)
    p = jax.nn.softmax(s, axis=-1).astype(mem_v.dtype)
    return jnp.einsum('bhtm,bmhd->bthd', p, mem_v).reshape(B, T, MEM_W)


def hierarchical_moe(x, wg, bg, we, be, w1, w3, w2):
    B, T, D = x.shape
    xt = x.reshape(-1, D)
    N = xt.shape[0]
    g_logits = (xt @ wg).astype(jnp.float32) + bg
    g_prob = jax.nn.softmax(g_logits, axis=-1)
    g_idx = jnp.argmax(g_logits, axis=-1)
    g_gate = jnp.take_along_axis(g_prob, g_idx[:, None], axis=-1)
    e_logits = ((xt @ we).astype(jnp.float32) + be).reshape(N, N_GROUPS, EXPERTS_PER_GROUP)
    e_logits = jnp.take_along_axis(e_logits, g_idx[:, None, None], axis=1)[:, 0]
    top_p, top_i = lax.top_k(jax.nn.softmax(e_logits, axis=-1), TOP_K)
    gate = (g_gate * top_p / top_p.sum(-1, keepdims=True)).reshape(-1)
    eid = (g_idx[:, None] * EXPERTS_PER_GROUP + top_i).reshape(-1)
    tok = jnp.repeat(jnp.arange(N), TOP_K)
    A = N * TOP_K
    order = jnp.argsort(eid)
    s_eid, s_tok, s_gate = eid[order], tok[order], gate[order]
    counts = jnp.bincount(eid, length=N_EXPERTS)
    start = jnp.cumsum(counts) - counts
    padded = (counts + MOE_BLOCK - 1) // MOE_BLOCK * MOE_BLOCK
    pad_end = jnp.cumsum(padded)
    pad_start = pad_end - padded
    dest = pad_start[s_eid] + jnp.arange(A) - start[s_eid]
    n_blocks = -(-(A + N_EXPERTS * (MOE_BLOCK - 1)) // MOE_BLOCK)
    P = n_blocks * MOE_BLOCK
    row_tok = jnp.zeros((P,), jnp.int32).at[dest].set(s_tok.astype(jnp.int32))
    row_gate = jnp.zeros((P,), jnp.float32).at[dest].set(s_gate)
    block_eid = jnp.minimum(jnp.searchsorted(pad_end, jnp.arange(n_blocks) * MOE_BLOCK, side='right'),
                            N_EXPERTS - 1)
    xb = xt[row_tok].reshape(n_blocks, MOE_BLOCK, D)

    def expert_block(args):
        xblk, e = args
        h = jax.nn.silu(xblk @ w1[e]) * (xblk @ w3[e])
        return h @ w2[e]

    yb = lax.map(expert_block, (xb, block_eid)).reshape(P, D)
    y = jax.ops.segment_sum(yb * row_gate[:, None].astype(yb.dtype), row_tok, num_segments=N)
    return y.reshape(B, T, D)


def setup_inputs(seed: int = 0) -> dict:
    key = jax.random.key(seed)
    ks = jax.random.split(key, 24)
    nrm = lambda k, shape, s: jax.random.normal(k, shape, jnp.float32) * s
    D = D_MODEL
    f_init = jnp.linspace(3.0, 6.0, ML_HEADS)
    ml_gate_b = jnp.concatenate([nrm(ks[7], (N_ML_LAYERS, 2, ML_HEADS), 0.1),
                                 f_init + nrm(ks[8], (N_ML_LAYERS, 2, ML_HEADS), 0.1)], axis=1)
    return {
        "x": nrm(ks[0], (BATCH, SEQ, D), 1.0),
        "mem": nrm(ks[1], (BATCH, N_MEM, D), 1.0),
        "w_mem_kv": nrm(ks[2], (D, 2 * MEM_W), D ** -0.5),
        "na_w_in": nrm(ks[3], (N_NA_LAYERS, D, NA_IN), D ** -0.5),
        "na_rpb": nrm(ks[4], (N_NA_LAYERS, NA_HEADS, 2 * WIN_H - 1, 2 * WIN_W - 1), 0.02),
        "ml_w_in": nrm(ks[5], (N_ML_LAYERS, D, ML_IN), D ** -0.5),
        "ml_conv_w": nrm(ks[6], (N_ML_LAYERS, CONV_K, 2 * TOKEN_W), CONV_K ** -0.5),
        "ml_conv_b": nrm(ks[9], (N_ML_LAYERS, 2 * TOKEN_W), 0.01),
        "ml_gate_b": ml_gate_b,
        "ml_norm_g": 1.0 + nrm(ks[10], (N_ML_LAYERS, TOKEN_W), 0.02),
        "w_out": nrm(ks[11], (DEPTH, MIX_W, D), MIX_W ** -0.5 * BETA),
        "ln1_g": 1.0 + nrm(ks[12], (DEPTH, D), 0.02),
        "ln1_b": nrm(ks[13], (DEPTH, D), 0.02),
        "ln2_g": 1.0 + nrm(ks[14], (DEPTH, D), 0.02),
        "ln2_b": nrm(ks[15], (DEPTH, D), 0.02),
        "moe_wg": nrm(ks[16], (DEPTH, D, N_GROUPS), D ** -0.5),
        "moe_bg": nrm(ks[17], (DEPTH, N_GROUPS), 0.01),
        "moe_we": nrm(ks[18], (DEPTH, D, N_EXPERTS), D ** -0.5),
        "moe_be": nrm(ks[19], (DEPTH, N_EXPERTS), 0.01),
        "moe_w1": nrm(ks[20], (DEPTH, N_EXPERTS, D, D_EXPERT), D ** -0.5),
        "moe_w3": nrm(ks[21], (DEPTH, N_EXPERTS, D, D_EXPERT), D ** -0.5),
        "moe_w2": nrm(ks[22], (DEPTH, N_EXPERTS, D_EXPERT, D), D_EXPERT ** -0.5 * BETA),
    }


def reference(x, mem, w_mem_kv, na_w_in, na_rpb, ml_w_in, ml_conv_w, ml_conv_b, ml_gate_b,
              ml_norm_g, w_out, ln1_g, ln1_b, ln2_g, ln2_b, moe_wg, moe_bg, moe_we, moe_be,
              moe_w1, moe_w3, moe_w2):
    B, T, D = x.shape
    mem_kv = (mem @ w_mem_kv).reshape(B, N_MEM, 2, MEM_HEADS, MEM_DH)
    mem_k, mem_v = mem_kv[:, :, 0], mem_kv[:, :, 1]
    for layer in range(DEPTH):
        j = layer // N_MIXERS
        if layer % N_MIXERS == 0:
            proj = x @ na_w_in[j]
            qkv = proj[..., :3 * TOKEN_W].reshape(B, T, 3, NA_HEADS, HEAD_DIM)
            tok = neighbourhood_attention(qkv[:, :, 0], qkv[:, :, 1], qkv[:, :, 2], na_rpb[j])
            q_mem = proj[..., 3 * TOKEN_W:]
        else:
            proj = x @ ml_w_in[j]
            tok = mlstm_mixer(proj[..., :ML_PROJ], ml_conv_w[j], ml_conv_b[j], ml_gate_b[j], ml_norm_g[j])
            q_mem = proj[..., ML_PROJ:]
        mixed = jnp.concatenate([tok, memory_attention(q_mem, mem_k, mem_v)], axis=-1)
        x = layer_norm(ALPHA * x + mixed @ w_out[layer], ln1_g[layer], ln1_b[layer])
        ffn = hierarchical_moe(x, moe_wg[layer], moe_bg[layer], moe_we[layer], moe_be[layer],
                               moe_w1[layer], moe_w3[layer], moe_w2[layer])
        x = layer_norm(ALPHA * x + ffn, ln2_g[layer], ln2_b[layer])
    return x
```

```python
import functools
import math

import jax
import jax.numpy as jnp
from jax import lax
from jax.experimental import pallas as pl
from jax.experimental.pallas import tpu as pltpu

D_MODEL = 1024
DEPTH = 4
N_MEM = 256
GRID_W = 64
TOKEN_W = 768
MEM_W = 256
HEAD_DIM = 64
NA_HEADS = 12
WIN_H = 8
WIN_W = 16
ML_HEADS = 4
ML_DH = 192
ML_CHUNK = 64
CONV_K = 5
MEM_HEADS = 4
MEM_DH = 64
N_GROUPS = 4
EXPERTS_PER_GROUP = 8
N_EXPERTS = 32
TOP_K = 2
D_EXPERT = 512
LN_EPS = 1e-5
ALPHA = (2 * DEPTH) ** 0.25

LANE = 128
ML_DH_PAD = 256
ML_W_PAD = ML_HEADS * ML_DH_PAD
VMEM_LIMIT_BYTES = 56 * 1024 * 1024

PROJ_TM = 512
PROJ_TN = 640
POST_TM = 256
MOE_BLK = 256
PERMUTE_CHUNK = 2048
NEG_BIG = -1e30

F32 = jnp.float32
BF16 = jnp.bfloat16


def _cparams(*sem):
    return pltpu.CompilerParams(dimension_semantics=sem, vmem_limit_bytes=VMEM_LIMIT_BYTES)


def _matmul_kernel(x_ref, w_ref, o_ref):
    x = x_ref[...].astype(BF16)
    o_ref[...] = jnp.dot(x, w_ref[...], preferred_element_type=F32).astype(o_ref.dtype)


def _matmul(x, w, out_dtype, tm, tn, name):
    m, k = x.shape
    n = w.shape[1]
    assert m % tm == 0 and n % tn == 0
    return pl.pallas_call(
        _matmul_kernel,
        grid=(m // tm, n // tn),
        in_specs=[pl.BlockSpec((tm, k), lambda i, j: (i, 0)),
                  pl.BlockSpec((k, tn), lambda i, j: (0, j))],
        out_specs=pl.BlockSpec((tm, tn), lambda i, j: (i, j)),
        out_shape=jax.ShapeDtypeStruct((m, n), out_dtype),
        compiler_params=_cparams("parallel", "arbitrary"),
        name=name,
    )(x, w)


def _na_bias_table(rpb, rows):
    kh = min(WIN_H, rows)
    off = jnp.arange(kh)[:, None]
    j = jnp.arange(kh)[None, :]
    ri = j - off + (WIN_H - 1)
    c = jnp.arange(GRID_W)[:, None]
    kc = jnp.arange(GRID_W)[None, :]
    cs = jnp.clip(c - WIN_W // 2, 0, GRID_W - WIN_W)
    valid = (kc >= cs) & (kc < cs + WIN_W)
    ci = jnp.clip(kc - c + (WIN_W - 1), 0, 2 * WIN_W - 2)
    tab = rpb[:, ri[:, None, :, None], ci[None, :, None, :]]
    tab = jnp.where(valid[None, None, :, None, :], tab, NEG_BIG)
    tab = tab.reshape(NA_HEADS // 2, 2, kh, GRID_W, kh * GRID_W)
    tab = tab.transpose(0, 2, 1, 3, 4).reshape(NA_HEADS // 2, kh, 2 * GRID_W, kh * GRID_W)
    return tab.astype(F32)


def _na_kernel(q_ref, k_ref, v_ref, b_ref, o_ref, *, rows, kh):
    lane = lax.broadcasted_iota(jnp.int32, (GRID_W, LANE), 1)
    first = lane < HEAD_DIM
    scale = HEAD_DIM ** -0.5

    def body(r, carry):
        rs = jnp.clip(r - kh // 2, 0, rows - kh)
        off = r - rs
        q = q_ref[0, pl.ds(pl.multiple_of(r * GRID_W, GRID_W), GRID_W), :]
        kb = k_ref[0, pl.ds(pl.multiple_of(rs * GRID_W, GRID_W), kh * GRID_W), :]
        vb = v_ref[0, pl.ds(pl.multiple_of(rs * GRID_W, GRID_W), kh * GRID_W), :]
        zero = jnp.zeros_like(q)
        qq = jnp.concatenate([jnp.where(first, q, zero), jnp.where(first, zero, q)], axis=0)
        s = lax.dot_general(qq, kb, (((1,), (1,)), ((), ())), preferred_element_type=F32)
        s = s * scale + b_ref[0, off]
        m = jnp.max(s, axis=-1, keepdims=True)
        p = jnp.exp(s - m)
        l = jnp.sum(p, axis=-1, keepdims=True)
        o = jnp.dot(p.astype(BF16), vb, preferred_element_type=F32) / l
        out = jnp.where(first, o[:GRID_W], o[GRID_W:])
        o_ref[0, pl.ds(pl.multiple_of(r * GRID_W, GRID_W), GRID_W), :] = out.astype(o_ref.dtype)
        return carry

    lax.fori_loop(0, rows, body, 0)


def _neighbourhood_attention(proj, bias_tab, batch, seq):
    rows = seq // GRID_W
    kh = min(WIN_H, rows)
    npair = NA_HEADS // 2
    kern = functools.partial(_na_kernel, rows=rows, kh=kh)
    return pl.pallas_call(
        kern,
        grid=(batch, npair),
        in_specs=[pl.BlockSpec((1, seq, LANE), lambda b, h: (b, 0, h)),
                  pl.BlockSpec((1, seq, LANE), lambda b, h: (b, 0, npair + h)),
                  pl.BlockSpec((1, seq, LANE), lambda b, h: (b, 0, 2 * npair + h)),
                  pl.BlockSpec((1, kh, 2 * GRID_W, kh * GRID_W), lambda b, h: (h, 0, 0, 0))],
        out_specs=pl.BlockSpec((1, seq, LANE), lambda b, h: (b, 0, h)),
        out_shape=jax.ShapeDtypeStruct((batch, seq, TOKEN_W), BF16),
        compiler_params=_cparams("parallel", "arbitrary"),
        name="na_attention",
    )(proj, proj, proj, bias_tab)


def _conv_silu_kernel(x_ref, w_ref, b_ref, s_ref, o_ref, *, seq):
    x = x_ref[0]
    t = lax.broadcasted_iota(jnp.int32, x.shape, 0)
    acc = jnp.zeros_like(x) + b_ref[...]
    for j in range(CONV_K):
        sh = CONV_K // 2 - j
        if sh == 0:
            xs = x
        elif sh > 0:
            xs = jnp.where(t >= sh, pltpu.roll(x, sh, 0), 0.0)
        else:
            xs = jnp.where(t < seq + sh, pltpu.roll(x, seq + sh, 0), 0.0)
        acc = acc + w_ref[j:j + 1, :] * xs
    y = acc * jax.nn.sigmoid(acc) * s_ref[...]
    o_ref[0] = y.astype(o_ref.dtype)


def _conv_silu(proj, conv_w, conv_b, col_scale, batch, seq):
    ncol = 2 * ML_W_PAD // LANE
    kern = functools.partial(_conv_silu_kernel, seq=seq)
    return pl.pallas_call(
        kern,
        grid=(batch, ncol),
        in_specs=[pl.BlockSpec((1, seq, LANE), lambda b, j: (b, 0, j)),
                  pl.BlockSpec((CONV_K, LANE), lambda b, j: (0, j)),
                  pl.BlockSpec((1, LANE), lambda b, j: (0, j)),
                  pl.BlockSpec((1, LANE), lambda b, j: (0, j))],
        out_specs=pl.BlockSpec((1, seq, LANE), lambda b, j: (b, 0, j)),
        out_shape=jax.ShapeDtypeStruct((batch, seq, 2 * ML_W_PAD), BF16),
        compiler_params=_cparams("parallel", "arbitrary"),
        name="ml_conv_silu",
    )(proj, conv_w, conv_b, col_scale)


def _log_sigmoid(x):
    return jnp.minimum(x, 0.0) - jnp.log1p(jnp.exp(-jnp.abs(x)))


def _mlstm_kernel(q_ref, k_ref, v_ref, g_ref, gb_ref, o_ref, c_ref, m_ref, *, reverse):
    L = ML_CHUNK
    d = 1 if reverse else 0

    @pl.when(pl.program_id(1) == 0)
    def _():
        c_ref[...] = jnp.zeros_like(c_ref)
        m_ref[...] = jnp.zeros_like(m_ref)

    g = g_ref[0] + gb_ref[...]
    glane = lax.broadcasted_iota(jnp.int32, g.shape, 1)
    is_f = (glane >= 2 * ML_HEADS) & (glane < 4 * ML_HEADS)
    act = jnp.where(is_f, _log_sigmoid(g), g)
    row = lax.broadcasted_iota(jnp.int32, (L, L), 0)
    col = lax.broadcasted_iota(jnp.int32, (L, L), 1)
    valid = (col >= row) if reverse else (col <= row)
    bcum = jnp.dot(valid.astype(F32), act, preferred_element_type=F32,
                   precision=lax.Precision.HIGHEST)
    act_t = act.T
    bcum_t = bcum.T
    last = 0 if reverse else L - 1
    hlane = lax.broadcasted_iota(jnp.int32, (L, ML_DH_PAD), 1)

    for h in range(ML_HEADS):
        li_lane = d * ML_HEADS + h
        lf_lane = 2 * ML_HEADS + d * ML_HEADS + h
        b_col = bcum[:, lf_lane:lf_lane + 1]
        b_row = bcum_t[lf_lane:lf_lane + 1, :]
        li_col = act[:, li_lane:li_lane + 1]
        li_row = act_t[li_lane:li_lane + 1, :]
        dm = jnp.where(valid, b_col - b_row + li_row, NEG_BIG)
        m_prev = m_ref[h][0:1, 0:1]
        inter = b_col + m_prev
        m_t = jnp.maximum(inter, jnp.max(dm, axis=-1, keepdims=True))
        w_inter = jnp.exp(inter - m_t)

        sl = slice(h * ML_DH_PAD, (h + 1) * ML_DH_PAD)
        qh = q_ref[0, :, sl]
        kh = k_ref[0, :, sl]
        vh = jnp.where(hlane == ML_DH, 1.0, v_ref[0, :, sl]).astype(BF16)
        s = lax.dot_general(qh, kh, (((1,), (1,)), ((), ())), preferred_element_type=F32)
        s = s * jnp.exp(dm - m_t)
        c_old = c_ref[h]
        num = (w_inter * jnp.dot(qh, c_old.astype(BF16), preferred_element_type=F32)
               + jnp.dot(s.astype(BF16), vh, preferred_element_type=F32))
        den = num[:, ML_DH:ML_DH + 1]
        hout = num / jnp.maximum(jnp.abs(den), jnp.exp(-m_t))
        o_ref[0, :, sl] = jnp.where(hlane < ML_DH, hout, 0.0)

        b_last = b_col[last:last + 1, :]
        m_new = m_t[last:last + 1, :]
        decay = jnp.exp(b_last + m_prev - m_new)
        w_src = jnp.exp(b_last - b_col + li_col - m_new)
        kw = (kh.astype(F32) * w_src).astype(BF16)
        upd = lax.dot_general(kw, vh, (((0,), (0,)), ((), ())), preferred_element_type=F32)
        c_ref[h] = decay * c_old + upd
        m_ref[h] = jnp.zeros(m_ref.shape[1:], F32) + m_new


def _mlstm_scan(qk, proj, gate_b, batch, seq, reverse):
    nc = seq // ML_CHUNK
    cidx = (lambda c: nc - 1 - c) if reverse else (lambda c: c)
    gcol = (4 * ML_W_PAD + MEM_W) // LANE
    kern = functools.partial(_mlstm_kernel, reverse=reverse)
    return pl.pallas_call(
        kern,
        grid=(batch, nc),
        in_specs=[pl.BlockSpec((1, ML_CHUNK, ML_W_PAD), lambda b, c: (b, cidx(c), 0)),
                  pl.BlockSpec((1, ML_CHUNK, ML_W_PAD), lambda b, c: (b, cidx(c), 1)),
                  pl.BlockSpec((1, ML_CHUNK, ML_W_PAD), lambda b, c: (b, cidx(c), 2)),
                  pl.BlockSpec((1, ML_CHUNK, LANE), lambda b, c: (b, cidx(c), gcol)),
                  pl.BlockSpec((1, LANE), lambda b, c: (0, 0))],
        out_specs=pl.BlockSpec((1, ML_CHUNK, ML_W_PAD), lambda b, c: (b, cidx(c), 0)),
        out_shape=jax.ShapeDtypeStruct((batch, seq, ML_W_PAD), F32),
        scratch_shapes=[pltpu.VMEM((ML_HEADS, ML_DH_PAD, ML_DH_PAD), F32),
                        pltpu.VMEM((ML_HEADS, 8, LANE), F32)],
        compiler_params=_cparams("parallel", "arbitrary"),
        name="mlstm_bwd" if reverse else "mlstm_fwd",
    )(qk, qk, proj, proj, gate_b)


def _layer_norm(y, g, b):
    mu = jnp.mean(y, axis=-1, keepdims=True)
    yc = y - mu
    var = jnp.mean(yc * yc, axis=-1, keepdims=True)
    return yc * lax.rsqrt(var + LN_EPS) * g + b


def _first_argmax(val, lane):
    mx = jnp.max(val, axis=-1, keepdims=True)
    idx = jnp.min(jnp.where(val == mx, lane, LANE), axis=-1, keepdims=True)
    return mx, idx


def _route(x1, wr_ref, br_ref):
    logits = jnp.dot(x1, wr_ref[...], preferred_element_type=F32,
                     precision=lax.Precision.HIGHEST) + br_ref[...]
    lane = lax.broadcasted_iota(jnp.int32, logits.shape, 1)
    is_g = lane < N_GROUPS
    gl = jnp.where(is_g, logits, NEG_BIG)
    gmax, gidx = _first_argmax(gl, lane)
    gsum = jnp.sum(jnp.where(is_g, jnp.exp(gl - gmax), 0.0), axis=-1, keepdims=True)
    g_gate = 1.0 / gsum
    lo = N_GROUPS + gidx * EXPERTS_PER_GROUP
    in_grp = (lane >= lo) & (lane < lo + EXPERTS_PER_GROUP)
    el = jnp.where(in_grp, logits, NEG_BIG)
    emax = jnp.max(el, axis=-1, keepdims=True)
    pe = jnp.where(in_grp, jnp.exp(el - emax), 0.0)
    prob = pe / jnp.sum(pe, axis=-1, keepdims=True)
    prob = jnp.where(in_grp, prob, -1.0)
    p1, i1 = _first_argmax(prob, lane)
    p2, i2 = _first_argmax(jnp.where(lane == i1, -1.0, prob), lane)
    tot = p1 + p2
    return i1 - N_GROUPS, i2 - N_GROUPS, g_gate * p1 / tot, g_gate * p2 / tot


def _post_kernel(*refs, ml):
    if ml:
        (hf_ref, hb_ref, op_ref, ng_ref, qm_ref, x_ref, mkv_ref, wo_ref, g_ref, b_ref, wr_ref, br_ref,
         x1_ref, ri_ref, rf_ref) = refs
        hsum = hf_ref[...] + hb_ref[...]
        hl = lax.broadcasted_iota(jnp.int32, (hsum.shape[0], ML_DH_PAD), 1)
        parts = []
        for h in range(ML_HEADS):
            hh = hsum[:, h * ML_DH_PAD:(h + 1) * ML_DH_PAD]
            mu = jnp.sum(hh, axis=-1, keepdims=True) * (1.0 / ML_DH)
            hc = jnp.where(hl < ML_DH, hh - mu, 0.0)
            var = jnp.sum(hc * hc, axis=-1, keepdims=True) * (1.0 / ML_DH)
            parts.append(hc * lax.rsqrt(var + LN_EPS))
        hn = jnp.concatenate(parts, axis=-1) * ng_ref[...]
        tok = (jax.nn.sigmoid(op_ref[...]) * hn).astype(BF16)
    else:
        (tok_ref, qm_ref, x_ref, mkv_ref, wo_ref, g_ref, b_ref, wr_ref, br_ref,
         x1_ref, ri_ref, rf_ref) = refs
        tok = tok_ref[...]

    qm = qm_ref[...].astype(BF16)
    mk = mkv_ref[0, :, 0:MEM_W]
    mv = mkv_ref[0, :, MEM_W:2 * MEM_W]
    mlane = lax.broadcasted_iota(jnp.int32, qm.shape, 1)
    matt = jnp.zeros(qm.shape, F32)
    for h in range(MEM_HEADS):
        hm = (mlane >= h * MEM_DH) & (mlane < (h + 1) * MEM_DH)
        qh = jnp.where(hm, qm, jnp.zeros_like(qm))
        s = lax.dot_general(qh, mk, (((1,), (1,)), ((), ())), preferred_element_type=F32) * (MEM_DH ** -0.5)
        p = jnp.exp(s - jnp.max(s, axis=-1, keepdims=True))
        l = jnp.sum(p, axis=-1, keepdims=True)
        o = jnp.dot(p.astype(BF16), mv, preferred_element_type=F32) / l
        matt = jnp.where(hm, o, matt)

    mixed = jnp.concatenate([tok, matt.astype(BF16)], axis=-1)
    y = ALPHA * x_ref[...] + jnp.dot(mixed, wo_ref[...], preferred_element_type=F32)
    x1 = _layer_norm(y, g_ref[...], b_ref[...])
    x1_ref[...] = x1

    e1, e2, g1, g2 = _route(x1, wr_ref, br_ref)
    lane = lax.broadcasted_iota(jnp.int32, ri_ref.shape, 1)
    ri_ref[...] = jnp.where(lane == 0, e1, jnp.where(lane == 1, e2, 0))
    rf_ref[...] = jnp.where(lane == 0, g1, jnp.where(lane == 1, g2, 0.0))


def _post_mixer(mix_inputs, x, mem_kv, w_out, ln_g, ln_b, w_route, b_route, seq, ml):
    n = x.shape[0]
    tm = POST_TM
    tiles_per_batch = seq // tm
    row = lambda i: (i, 0)
    const = lambda i: (0, 0)
    if ml:
        h_f, h_b, proj, norm_g = mix_inputs
        ins = [h_f, h_b, proj, norm_g, proj]
        specs = [pl.BlockSpec((tm, ML_W_PAD), row),
                 pl.BlockSpec((tm, ML_W_PAD), row),
                 pl.BlockSpec((tm, ML_W_PAD), lambda i: (i, 3)),
                 pl.BlockSpec((1, ML_W_PAD), const),
                 pl.BlockSpec((tm, MEM_W), lambda i: (i, 4 * ML_W_PAD // MEM_W))]
    else:
        tok, proj = mix_inputs
        ins = [tok, proj]
        specs = [pl.BlockSpec((tm, TOKEN_W), row),
                 pl.BlockSpec((tm, MEM_W), lambda i: (i, 3 * TOKEN_W // MEM_W))]
    ins += [x, mem_kv, w_out, ln_g, ln_b, w_route, b_route]
    specs += [pl.BlockSpec((tm, D_MODEL), row),
              pl.BlockSpec((1, N_MEM, 2 * MEM_W), lambda i: (i // tiles_per_batch, 0, 0)),
              pl.BlockSpec(w_out.shape, const),
              pl.BlockSpec((1, D_MODEL), const),
              pl.BlockSpec((1, D_MODEL), const),
              pl.BlockSpec((D_MODEL, LANE), const),
              pl.BlockSpec((1, LANE), const)]
    return pl.pallas_call(
        functools.partial(_post_kernel, ml=ml),
        grid=(n // tm,),
        in_specs=specs,
        out_specs=[pl.BlockSpec((tm, D_MODEL), row),
                   pl.BlockSpec((tm, LANE), row),
                   pl.BlockSpec((tm, LANE), row)],
        out_shape=[jax.ShapeDtypeStruct((n, D_MODEL), F32),
                   jax.ShapeDtypeStruct((n, LANE), jnp.int32),
                   jax.ShapeDtypeStruct((n, LANE), F32)],
        compiler_params=_cparams("parallel"),
        name="post_mixer_ml" if ml else "post_mixer_na",
    )(*ins)


def _permute_kernel(dest_ref, src_ref, init_ref, dst_ref, sem, *, dispatch):
    del init_ref
    base = pl.program_id(0) * PERMUTE_CHUNK

    def copy(a):
        slot = dest_ref[a]
        if dispatch:
            tok = lax.shift_right_logical(a, 1)
            return pltpu.make_async_copy(src_ref.at[pl.ds(tok, 1)], dst_ref.at[pl.ds(slot, 1)], sem)
        return pltpu.make_async_copy(src_ref.at[pl.ds(slot, 1)], dst_ref.at[pl.ds(a, 1)], sem)

    def start(i, c):
        copy(base + i).start()
        return c

    def wait(i, c):
        copy(base + i).wait()
        return c

    lax.fori_loop(0, PERMUTE_CHUNK, start, 0, unroll=8)
    lax.fori_loop(0, PERMUTE_CHUNK, wait, 0, unroll=8)


def _row_permute(dest, src, init, dispatch):
    n_assign = dest.shape[0]
    assert n_assign % PERMUTE_CHUNK == 0
    return pl.pallas_call(
        functools.partial(_permute_kernel, dispatch=dispatch),
        grid_spec=pltpu.PrefetchScalarGridSpec(
            num_scalar_prefetch=1,
            grid=(n_assign // PERMUTE_CHUNK,),
            in_specs=[pl.BlockSpec(memory_space=pl.ANY), pl.BlockSpec(memory_space=pl.ANY)],
            out_specs=pl.BlockSpec(memory_space=pl.ANY),
            scratch_shapes=[pltpu.SemaphoreType.DMA]),
        out_shape=jax.ShapeDtypeStruct(init.shape, init.dtype),
        input_output_aliases={2: 0},
        compiler_params=_cparams("arbitrary"),
        name="moe_dispatch" if dispatch else "moe_combine",
    )(dest, src, init)


def _expert_kernel(beid_ref, nused_ref, x_ref, w1_ref, w3_ref, w2_ref, o_ref, w1b, w3b, w2b):
    i = pl.program_id(0)
    prev = beid_ref[jnp.maximum(i - 1, 0)]
    fresh = (i == 0) | (beid_ref[i] != prev)

    @pl.when(fresh)
    def _():
        w1b[...] = w1_ref[...].astype(BF16)
        w3b[...] = w3_ref[...].astype(BF16)
        w2b[...] = w2_ref[...].astype(BF16)

    @pl.when(i < nused_ref[0])
    def _():
        x = x_ref[...].astype(BF16)
        a = jnp.dot(x, w1b[...], preferred_element_type=F32)
        b = jnp.dot(x, w3b[...], preferred_element_type=F32)
        hmid = (a * jax.nn.sigmoid(a) * b).astype(BF16)
        o_ref[...] = jnp.dot(hmid, w2b[...], preferred_element_type=F32)

    @pl.when(i >= nused_ref[0])
    def _():
        o_ref[...] = jnp.zeros_like(o_ref)


def _expert_mlp(block_eid, n_used, xs, w1, w3, w2, layer):
    p = xs.shape[0]
    nb = p // MOE_BLK
    wmap13 = lambda i, be, nu: (layer, be[i], 0, 0)
    return pl.pallas_call(
        _expert_kernel,
        grid_spec=pltpu.PrefetchScalarGridSpec(
            num_scalar_prefetch=2,
            grid=(nb,),
            in_specs=[pl.BlockSpec((MOE_BLK, D_MODEL), lambda i, be, nu: (i, 0)),
                      pl.BlockSpec((None, None, D_MODEL, D_EXPERT), wmap13),
                      pl.BlockSpec((None, None, D_MODEL, D_EXPERT), wmap13),
                      pl.BlockSpec((None, None, D_EXPERT, D_MODEL), wmap13)],
            out_specs=pl.BlockSpec((MOE_BLK, D_MODEL), lambda i, be, nu: (i, 0)),
            scratch_shapes=[pltpu.VMEM((D_MODEL, D_EXPERT), BF16),
                            pltpu.VMEM((D_MODEL, D_EXPERT), BF16),
                            pltpu.VMEM((D_EXPERT, D_MODEL), BF16)]),
        out_shape=jax.ShapeDtypeStruct((p, D_MODEL), F32),
        compiler_params=_cparams("arbitrary"),
        name="moe_experts",
    )(block_eid, n_used, xs, w1, w3, w2)


def _ffn_norm_kernel(x_ref, y_ref, rf_ref, g_ref, b_ref, o_ref):
    g1 = rf_ref[:, 0:1]
    g2 = rf_ref[:, 1:2]
    ffn = y_ref[:, 0:D_MODEL] * g1 + y_ref[:, D_MODEL:2 * D_MODEL] * g2
    o_ref[...] = _layer_norm(ALPHA * x_ref[...] + ffn, g_ref[...], b_ref[...])


def _ffn_norm(x1, y2, route_f, ln_g, ln_b):
    n = x1.shape[0]
    tm = POST_TM
    row = lambda i: (i, 0)
    const = lambda i: (0, 0)
    return pl.pallas_call(
        _ffn_norm_kernel,
        grid=(n // tm,),
        in_specs=[pl.BlockSpec((tm, D_MODEL), row),
                  pl.BlockSpec((tm, TOP_K * D_MODEL), row),
                  pl.BlockSpec((tm, LANE), row),
                  pl.BlockSpec((1, D_MODEL), const),
                  pl.BlockSpec((1, D_MODEL), const)],
        out_specs=pl.BlockSpec((tm, D_MODEL), row),
        out_shape=jax.ShapeDtypeStruct((n, D_MODEL), F32),
        compiler_params=_cparams("parallel"),
        name="ffn_norm",
    )(x1, y2, route_f, ln_g, ln_b)


def _routing_plan(eid):
    n = eid.shape[0]
    a = n * TOP_K
    n_blocks = -(-(a + N_EXPERTS * (MOE_BLK - 1)) // MOE_BLK)
    e = eid.reshape(a)
    onehot = (e[:, None] == jnp.arange(N_EXPERTS, dtype=jnp.int32)[None, :]).astype(jnp.int32)
    csum = jnp.cumsum(onehot, axis=0)
    counts = csum[-1]
    rank = jnp.sum((csum - onehot) * onehot, axis=1)
    padded = (counts + MOE_BLK - 1) // MOE_BLK * MOE_BLK
    pad_end = jnp.cumsum(padded)
    pad_start = pad_end - padded
    dest = (pad_start[e] + rank).astype(jnp.int32)
    block_eid = jnp.minimum(
        jnp.searchsorted(pad_end, jnp.arange(n_blocks, dtype=jnp.int32) * MOE_BLK, side='right'),
        N_EXPERTS - 1).astype(jnp.int32)
    n_used = (pad_end[-1] // MOE_BLK).astype(jnp.int32).reshape(1)
    return dest, block_eid, n_used, n_blocks


def _moe(x1, route_i, route_f, w1, w3, w2, ln_g, ln_b, layer):
    n = x1.shape[0]
    dest, block_eid, n_used, n_blocks = _routing_plan(route_i[:, :TOP_K])
    xs = _row_permute(dest, x1, jnp.zeros((n_blocks * MOE_BLK, D_MODEL), F32), dispatch=True)
    ys = _expert_mlp(block_eid, n_used, xs, w1, w3, w2, layer)
    y2 = _row_permute(dest, ys, jnp.zeros((n * TOP_K, D_MODEL), F32), dispatch=False)
    return _ffn_norm(x1, y2.reshape(n, TOP_K * D_MODEL), route_f, ln_g, ln_b)


def _pad_heads(w):
    lead = w.shape[:-1]
    w = w.reshape(*lead, ML_HEADS, ML_DH)
    w = jnp.pad(w, [(0, 0)] * len(lead) + [(0, 0), (0, ML_DH_PAD - ML_DH)])
    return w.reshape(*lead, ML_W_PAD)


def _ml_in_weight(w):
    tw = TOKEN_W
    segs = [_pad_heads(w[:, i * tw:(i + 1) * tw]) for i in range(4)]
    gates = w[:, 4 * tw:4 * tw + 4 * ML_HEADS]
    qmem = w[:, 4 * tw + 4 * ML_HEADS:]
    gates = jnp.pad(gates, ((0, 0), (0, LANE - 4 * ML_HEADS)))
    return jnp.concatenate(segs + [qmem, gates], axis=1)


def _route_weight(wg, bg, we, be):
    w = jnp.concatenate([wg, we], axis=1)
    b = jnp.concatenate([bg, be], axis=0)
    pad = LANE - w.shape[1]
    return jnp.pad(w, ((0, 0), (0, pad))).astype(F32), jnp.pad(b, (0, pad)).reshape(1, LANE).astype(F32)


def kernel(x, mem, w_mem_kv, na_w_in, na_rpb, ml_w_in, ml_conv_w, ml_conv_b, ml_gate_b, ml_norm_g, w_out,
           ln1_g, ln1_b, ln2_g, ln2_b, moe_wg, moe_bg, moe_we, moe_be, moe_w1, moe_w3, moe_w2):
    batch, seq, d = x.shape
    n = batch * seq
    rows = seq // GRID_W
    xf = x.reshape(n, d)

    mem_kv = _matmul(mem.reshape(batch * N_MEM, d), w_mem_kv.astype(BF16), BF16,
                     min(PROJ_TM, batch * N_MEM), 2 * MEM_W, "mem_kv").reshape(batch, N_MEM, 2 * MEM_W)

    for layer in range(DEPTH):
        j = layer // 2
        w_route, b_route = _route_weight(moe_wg[layer], moe_bg[layer], moe_we[layer], moe_be[layer])
        g1 = ln1_g[layer].reshape(1, d)
        b1 = ln1_b[layer].reshape(1, d)
        if layer % 2 == 0:
            proj = _matmul(xf, na_w_in[j].astype(BF16), BF16, PROJ_TM, PROJ_TN, "na_in_proj")
            tok = _neighbourhood_attention(proj.reshape(batch, seq, -1), _na_bias_table(na_rpb[j], rows),
                                           batch, seq)
            x1, route_i, route_f = _post_mixer((tok.reshape(n, TOKEN_W), proj), xf, mem_kv,
                                               w_out[layer].astype(BF16), g1, b1, w_route, b_route, seq, ml=False)
        else:
            proj = _matmul(xf, _ml_in_weight(ml_w_in[j]).astype(BF16), F32, PROJ_TM, PROJ_TN, "ml_in_proj")
            proj3 = proj.reshape(batch, seq, -1)
            conv_w = jnp.concatenate([_pad_heads(ml_conv_w[j][:, :TOKEN_W]),
                                      _pad_heads(ml_conv_w[j][:, TOKEN_W:])], axis=1)
            conv_b = jnp.concatenate([_pad_heads(ml_conv_b[j][:TOKEN_W]),
                                      _pad_heads(ml_conv_b[j][TOKEN_W:])]).reshape(1, 2 * ML_W_PAD)
            col_scale = jnp.concatenate([jnp.full((ML_W_PAD,), ML_DH ** -0.5, F32),
                                         jnp.ones((ML_W_PAD,), F32)]).reshape(1, 2 * ML_W_PAD)
            qk = _conv_silu(proj3, conv_w, conv_b, col_scale, batch, seq)
            gate_b = jnp.pad(ml_gate_b[j].reshape(-1), (0, LANE - 4 * ML_HEADS)).reshape(1, LANE)
            h_f = _mlstm_scan(qk, proj3, gate_b, batch, seq, reverse=False)
            h_b = _mlstm_scan(qk, proj3, gate_b, batch, seq, reverse=True)
            norm_g = _pad_heads(ml_norm_g[j]).reshape(1, ML_W_PAD)
            wo = jnp.concatenate([_pad_heads(w_out[layer][:TOKEN_W].T).T, w_out[layer][TOKEN_W:]], axis=0)
            x1, route_i, route_f = _post_mixer((h_f.reshape(n, ML_W_PAD), h_b.reshape(n, ML_W_PAD), proj, norm_g),
                                               xf, mem_kv, wo.astype(BF16), g1, b1, w_route, b_route, seq, ml=True)
        xf = _moe(x1, route_i, route_f, moe_w1, moe_w3, moe_w2,
                  ln2_g[layer].reshape(1, d), ln2_b[layer].reshape(1, d), layer)
    return xf.reshape(batch, seq, d)
```

```python
import functools
import math

import jax
import jax.numpy as jnp
from jax import lax
from jax.experimental import pallas as pl
from jax.experimental.pallas import tpu as pltpu

D_MODEL = 1024
DEPTH = 4
N_MEM = 256
GRID_W = 64
TOKEN_W = 768
MEM_W = 256
HEAD_DIM = 64
NA_HEADS = 12
WIN_H = 8
WIN_W = 16
ML_HEADS = 4
ML_DH = 192
ML_CHUNK = 64
CONV_K = 5
MEM_HEADS = 4
MEM_DH = 64
N_GROUPS = 4
EXPERTS_PER_GROUP = 8
N_EXPERTS = 32
TOP_K = 2
D_EXPERT = 512
LN_EPS = 1e-5
ALPHA = (2 * DEPTH) ** 0.25

LANE = 128
ML_DH_PAD = 256
ML_W_PAD = ML_HEADS * ML_DH_PAD
VMEM_LIMIT_BYTES = 56 * 1024 * 1024

PROJ_TM = 512
PROJ_TN = 640
POST_TM = 512
MOE_BLK = 256
SEG_ALIGN = 8
SEG_BITS = (TOP_K * POST_TM // SEG_ALIGN).bit_length()
LBUF = -(-(TOP_K * POST_TM + N_EXPERTS * (SEG_ALIGN - 1)) // 256) * 256
NEG_BIG = -1e30

F32 = jnp.float32
BF16 = jnp.bfloat16


def _cparams(*sem):
    return pltpu.CompilerParams(dimension_semantics=sem, vmem_limit_bytes=VMEM_LIMIT_BYTES)


def _matmul_kernel(x_ref, w_ref, o_ref):
    x = x_ref[...].astype(BF16)
    o_ref[...] = jnp.dot(x, w_ref[...], preferred_element_type=F32).astype(o_ref.dtype)


def _matmul(x, w, out_dtype, tm, tn, name):
    m, k = x.shape
    n = w.shape[1]
    assert m % tm == 0 and n % tn == 0
    return pl.pallas_call(
        _matmul_kernel,
        grid=(m // tm, n // tn),
        in_specs=[pl.BlockSpec((tm, k), lambda i, j: (i, 0)),
                  pl.BlockSpec((k, tn), lambda i, j: (0, j))],
        out_specs=pl.BlockSpec((tm, tn), lambda i, j: (i, j)),
        out_shape=jax.ShapeDtypeStruct((m, n), out_dtype),
        compiler_params=_cparams("parallel", "arbitrary"),
        name=name,
    )(x, w)


def _na_bias_table(rpb, rows):
    kh = min(WIN_H, rows)
    off = jnp.arange(kh)[:, None]
    j = jnp.arange(kh)[None, :]
    ri = j - off + (WIN_H - 1)
    c = jnp.arange(GRID_W)[:, None]
    kc = jnp.arange(GRID_W)[None, :]
    cs = jnp.clip(c - WIN_W // 2, 0, GRID_W - WIN_W)
    valid = (kc >= cs) & (kc < cs + WIN_W)
    ci = kc - c + (WIN_W - 1)
    row_sel = (ri[:, :, None] == jnp.arange(2 * WIN_H - 1)).astype(F32)
    col_sel = (ci[:, :, None] == jnp.arange(2 * WIN_W - 1)).astype(F32)
    tab = jnp.einsum('hab,oja,ckb->hocjk', rpb.astype(F32), row_sel, col_sel,
                     precision=lax.Precision.HIGHEST)
    tab = jnp.where(valid[None, None, :, None, :], tab, NEG_BIG)
    tab = tab.reshape(NA_HEADS // 2, 2, kh, GRID_W, kh * GRID_W)
    tab = tab.transpose(0, 2, 1, 3, 4).reshape(NA_HEADS // 2, kh, 2 * GRID_W, kh * GRID_W)
    return tab.astype(F32)


def _na_kernel(q_ref, k_ref, v_ref, b_ref, o_ref, *, rows, kh):
    lane = lax.broadcasted_iota(jnp.int32, (GRID_W, LANE), 1)
    first = lane < HEAD_DIM
    scale = HEAD_DIM ** -0.5

    def body(r, carry):
        rs = jnp.clip(r - kh // 2, 0, rows - kh)
        off = r - rs
        q = q_ref[0, pl.ds(pl.multiple_of(r * GRID_W, GRID_W), GRID_W), :]
        kb = k_ref[0, pl.ds(pl.multiple_of(rs * GRID_W, GRID_W), kh * GRID_W), :]
        vb = v_ref[0, pl.ds(pl.multiple_of(rs * GRID_W, GRID_W), kh * GRID_W), :]
        zero = jnp.zeros_like(q)
        qq = jnp.concatenate([jnp.where(first, q, zero), jnp.where(first, zero, q)], axis=0)
        s = lax.dot_general(qq, kb, (((1,), (1,)), ((), ())), preferred_element_type=F32)
        s = s * scale + b_ref[0, off]
        m = jnp.max(s, axis=-1, keepdims=True)
        p = jnp.exp(s - m)
        l = jnp.sum(p, axis=-1, keepdims=True)
        o = jnp.dot(p.astype(BF16), vb, preferred_element_type=F32) / l
        out = jnp.where(first, o[:GRID_W], o[GRID_W:])
        o_ref[0, pl.ds(pl.multiple_of(r * GRID_W, GRID_W), GRID_W), :] = out.astype(o_ref.dtype)
        return carry

    lax.fori_loop(0, rows, body, 0)


def _neighbourhood_attention(proj, bias_tab, batch, seq):
    rows = seq // GRID_W
    kh = min(WIN_H, rows)
    npair = NA_HEADS // 2
    kern = functools.partial(_na_kernel, rows=rows, kh=kh)
    return pl.pallas_call(
        kern,
        grid=(batch, npair),
        in_specs=[pl.BlockSpec((1, seq, LANE), lambda b, h: (b, 0, h)),
                  pl.BlockSpec((1, seq, LANE), lambda b, h: (b, 0, npair + h)),
                  pl.BlockSpec((1, seq, LANE), lambda b, h: (b, 0, 2 * npair + h)),
                  pl.BlockSpec((1, kh, 2 * GRID_W, kh * GRID_W), lambda b, h: (h, 0, 0, 0))],
        out_specs=pl.BlockSpec((1, seq, LANE), lambda b, h: (b, 0, h)),
        out_shape=jax.ShapeDtypeStruct((batch, seq, TOKEN_W), BF16),
        compiler_params=_cparams("parallel", "arbitrary"),
        name="na_attention",
    )(proj, proj, proj, bias_tab)


def _conv_silu_kernel(x_ref, w_ref, b_ref, s_ref, o_ref, *, seq):
    x = x_ref[0]
    t = lax.broadcasted_iota(jnp.int32, x.shape, 0)
    acc = jnp.zeros_like(x) + b_ref[...]
    for j in range(CONV_K):
        sh = CONV_K // 2 - j
        if sh == 0:
            xs = x
        elif sh > 0:
            xs = jnp.where(t >= sh, pltpu.roll(x, sh, 0), 0.0)
        else:
            xs = jnp.where(t < seq + sh, pltpu.roll(x, seq + sh, 0), 0.0)
        acc = acc + w_ref[j:j + 1, :] * xs
    y = acc * jax.nn.sigmoid(acc) * s_ref[...]
    o_ref[0] = y.astype(o_ref.dtype)


def _conv_silu(proj, conv_w, conv_b, col_scale, batch, seq):
    ncol = 2 * ML_W_PAD // LANE
    kern = functools.partial(_conv_silu_kernel, seq=seq)
    return pl.pallas_call(
        kern,
        grid=(batch, ncol),
        in_specs=[pl.BlockSpec((1, seq, LANE), lambda b, j: (b, 0, j)),
                  pl.BlockSpec((CONV_K, LANE), lambda b, j: (0, j)),
                  pl.BlockSpec((1, LANE), lambda b, j: (0, j)),
                  pl.BlockSpec((1, LANE), lambda b, j: (0, j))],
        out_specs=pl.BlockSpec((1, seq, LANE), lambda b, j: (b, 0, j)),
        out_shape=jax.ShapeDtypeStruct((batch, seq, 2 * ML_W_PAD), BF16),
        compiler_params=_cparams("parallel", "arbitrary"),
        name="ml_conv_silu",
    )(proj, conv_w, conv_b, col_scale)


def _log_sigmoid(x):
    return jnp.minimum(x, 0.0) - jnp.log1p(jnp.exp(-jnp.abs(x)))


def _mlstm_kernel(q_ref, k_ref, v_ref, g_ref, gb_ref, o_ref, c_ref, m_ref, *, reverse):
    L = ML_CHUNK
    d = 1 if reverse else 0

    @pl.when(pl.program_id(1) == 0)
    def _():
        c_ref[...] = jnp.zeros_like(c_ref)
        m_ref[...] = jnp.zeros_like(m_ref)

    g = g_ref[0] + gb_ref[...]
    glane = lax.broadcasted_iota(jnp.int32, g.shape, 1)
    is_f = (glane >= 2 * ML_HEADS) & (glane < 4 * ML_HEADS)
    act = jnp.where(is_f, _log_sigmoid(g), g)
    row = lax.broadcasted_iota(jnp.int32, (L, L), 0)
    col = lax.broadcasted_iota(jnp.int32, (L, L), 1)
    valid = (col >= row) if reverse else (col <= row)
    bcum = jnp.dot(valid.astype(F32), act, preferred_element_type=F32,
                   precision=lax.Precision.HIGHEST)
    act_t = act.T
    bcum_t = bcum.T
    last = 0 if reverse else L - 1
    hlane = lax.broadcasted_iota(jnp.int32, (L, ML_DH_PAD), 1)

    for h in range(ML_HEADS):
        li_lane = d * ML_HEADS + h
        lf_lane = 2 * ML_HEADS + d * ML_HEADS + h
        b_col = bcum[:, lf_lane:lf_lane + 1]
        b_row = bcum_t[lf_lane:lf_lane + 1, :]
        li_col = act[:, li_lane:li_lane + 1]
        li_row = act_t[li_lane:li_lane + 1, :]
        dm = jnp.where(valid, b_col - b_row + li_row, NEG_BIG)
        m_prev = m_ref[h][0:1, 0:1]
        inter = b_col + m_prev
        m_t = jnp.maximum(inter, jnp.max(dm, axis=-1, keepdims=True))
        w_inter = jnp.exp(inter - m_t)

        sl = slice(h * ML_DH_PAD, (h + 1) * ML_DH_PAD)
        qh = q_ref[0, :, sl]
        kh = k_ref[0, :, sl]
        vh = jnp.where(hlane == ML_DH, 1.0, v_ref[0, :, sl]).astype(BF16)
        s = lax.dot_general(qh, kh, (((1,), (1,)), ((), ())), preferred_element_type=F32)
        s = s * jnp.exp(dm - m_t)
        c_old = c_ref[h]
        num = (w_inter * jnp.dot(qh, c_old.astype(BF16), preferred_element_type=F32)
               + jnp.dot(s.astype(BF16), vh, preferred_element_type=F32))
        den = num[:, ML_DH:ML_DH + 1]
        hout = num / jnp.maximum(jnp.abs(den), jnp.exp(-m_t))
        o_ref[0, :, sl] = jnp.where(hlane < ML_DH, hout, 0.0)

        b_last = b_col[last:last + 1, :]
        m_new = m_t[last:last + 1, :]
        decay = jnp.exp(b_last + m_prev - m_new)
        w_src = jnp.exp(b_last - b_col + li_col - m_new)
        kw = (kh.astype(F32) * w_src).astype(BF16)
        upd = lax.dot_general(kw, vh, (((0,), (0,)), ((), ())), preferred_element_type=F32)
        c_ref[h] = decay * c_old + upd
        m_ref[h] = jnp.zeros(m_ref.shape[1:], F32) + m_new


def _mlstm_scan(qk, proj, gate_b, batch, seq, reverse):
    nc = seq // ML_CHUNK
    cidx = (lambda c: nc - 1 - c) if reverse else (lambda c: c)
    gcol = (4 * ML_W_PAD + MEM_W) // LANE
    kern = functools.partial(_mlstm_kernel, reverse=reverse)
    return pl.pallas_call(
        kern,
        grid=(batch, nc),
        in_specs=[pl.BlockSpec((1, ML_CHUNK, ML_W_PAD), lambda b, c: (b, cidx(c), 0)),
                  pl.BlockSpec((1, ML_CHUNK, ML_W_PAD), lambda b, c: (b, cidx(c), 1)),
                  pl.BlockSpec((1, ML_CHUNK, ML_W_PAD), lambda b, c: (b, cidx(c), 2)),
                  pl.BlockSpec((1, ML_CHUNK, LANE), lambda b, c: (b, cidx(c), gcol)),
                  pl.BlockSpec((1, LANE), lambda b, c: (0, 0))],
        out_specs=pl.BlockSpec((1, ML_CHUNK, ML_W_PAD), lambda b, c: (b, cidx(c), 0)),
        out_shape=jax.ShapeDtypeStruct((batch, seq, ML_W_PAD), F32),
        scratch_shapes=[pltpu.VMEM((ML_HEADS, ML_DH_PAD, ML_DH_PAD), F32),
                        pltpu.VMEM((ML_HEADS, 8, LANE), F32)],
        compiler_params=_cparams("parallel", "arbitrary"),
        name="mlstm_bwd" if reverse else "mlstm_fwd",
    )(qk, qk, proj, proj, gate_b)


def _layer_norm(y, g, b):
    mu = jnp.mean(y, axis=-1, keepdims=True)
    yc = y - mu
    var = jnp.mean(yc * yc, axis=-1, keepdims=True)
    return yc * lax.rsqrt(var + LN_EPS) * g + b


def _first_argmax(val, lane):
    mx = jnp.max(val, axis=-1, keepdims=True)
    idx = jnp.min(jnp.where(val == mx, lane, LANE), axis=-1, keepdims=True)
    return mx, idx


def _route(x1, wr_ref, br_ref):
    logits = jnp.dot(x1, wr_ref[...], preferred_element_type=F32,
                     precision=lax.Precision.HIGHEST) + br_ref[...]
    lane = lax.broadcasted_iota(jnp.int32, logits.shape, 1)
    is_g = lane < N_GROUPS
    gl = jnp.where(is_g, logits, NEG_BIG)
    gmax, gidx = _first_argmax(gl, lane)
    gsum = jnp.sum(jnp.where(is_g, jnp.exp(gl - gmax), 0.0), axis=-1, keepdims=True)
    g_gate = 1.0 / gsum
    lo = N_GROUPS + gidx * EXPERTS_PER_GROUP
    in_grp = (lane >= lo) & (lane < lo + EXPERTS_PER_GROUP)
    el = jnp.where(in_grp, logits, NEG_BIG)
    emax = jnp.max(el, axis=-1, keepdims=True)
    pe = jnp.where(in_grp, jnp.exp(el - emax), 0.0)
    prob = pe / jnp.sum(pe, axis=-1, keepdims=True)
    prob = jnp.where(in_grp, prob, -1.0)
    p1, i1 = _first_argmax(prob, lane)
    p2, i2 = _first_argmax(jnp.where(lane == i1, -1.0, prob), lane)
    tot = p1 + p2
    return i1 - N_GROUPS, i2 - N_GROUPS, g_gate * p1 / tot, g_gate * p2 / tot


def _post_kernel(*refs, ml):
    if ml:
        (hf_ref, hb_ref, op_ref, ng_ref, qm_ref, x_ref, mkv_ref, wo_ref, g_ref, b_ref, wr_ref, br_ref,
         x1_ref, ri_ref, rf_ref, cnt_ref) = refs
        hsum = hf_ref[...] + hb_ref[...]
        hl = lax.broadcasted_iota(jnp.int32, (hsum.shape[0], ML_DH_PAD), 1)
        parts = []
        for h in range(ML_HEADS):
            hh = hsum[:, h * ML_DH_PAD:(h + 1) * ML_DH_PAD]
            mu = jnp.sum(hh, axis=-1, keepdims=True) * (1.0 / ML_DH)
            hc = jnp.where(hl < ML_DH, hh - mu, 0.0)
            var = jnp.sum(hc * hc, axis=-1, keepdims=True) * (1.0 / ML_DH)
            parts.append(hc * lax.rsqrt(var + LN_EPS))
        hn = jnp.concatenate(parts, axis=-1) * ng_ref[...]
        tok = (jax.nn.sigmoid(op_ref[...]) * hn).astype(BF16)
    else:
        (tok_ref, qm_ref, x_ref, mkv_ref, wo_ref, g_ref, b_ref, wr_ref, br_ref,
         x1_ref, ri_ref, rf_ref, cnt_ref) = refs
        tok = tok_ref[...]

    qm = qm_ref[...].astype(BF16)
    mk = mkv_ref[0, :, 0:MEM_W]
    mv = mkv_ref[0, :, MEM_W:2 * MEM_W]
    mlane = lax.broadcasted_iota(jnp.int32, qm.shape, 1)
    matt = jnp.zeros(qm.shape, F32)
    for h in range(MEM_HEADS):
        hm = (mlane >= h * MEM_DH) & (mlane < (h + 1) * MEM_DH)
        qh = jnp.where(hm, qm, jnp.zeros_like(qm))
        s = lax.dot_general(qh, mk, (((1,), (1,)), ((), ())), preferred_element_type=F32) * (MEM_DH ** -0.5)
        p = jnp.exp(s - jnp.max(s, axis=-1, keepdims=True))
        l = jnp.sum(p, axis=-1, keepdims=True)
        o = jnp.dot(p.astype(BF16), mv, preferred_element_type=F32) / l
        matt = jnp.where(hm, o, matt)

    mixed = jnp.concatenate([tok, matt.astype(BF16)], axis=-1)
    y = ALPHA * x_ref[...] + jnp.dot(mixed, wo_ref[...], preferred_element_type=F32)
    x1 = _layer_norm(y, g_ref[...], b_ref[...])
    x1_ref[...] = x1

    e1, e2, g1, g2 = _route(x1, wr_ref, br_ref)
    lane = lax.broadcasted_iota(jnp.int32, ri_ref.shape, 1)
    tm = x1.shape[0]
    oh1 = jnp.where(lane == e1, 1.0, 0.0)
    oh2 = jnp.where(lane == e2, 1.0, 0.0)
    trow = lax.broadcasted_iota(jnp.int32, (tm, tm), 0)
    tcol = lax.broadcasted_iota(jnp.int32, (tm, tm), 1)
    earlier = jnp.where(tcol < trow, 1.0, 0.0).astype(BF16)
    before1 = jnp.dot(earlier, oh1.astype(BF16), preferred_element_type=F32)
    before2 = jnp.dot(earlier, oh2.astype(BF16), preferred_element_type=F32)
    c1 = jnp.sum(oh1, axis=0, keepdims=True)
    c2 = jnp.sum(oh2, axis=0, keepdims=True)
    rank1 = jnp.sum(oh1 * before1, axis=-1, keepdims=True).astype(jnp.int32)
    rank2 = jnp.sum(oh2 * (before2 + c1), axis=-1, keepdims=True).astype(jnp.int32)
    ri_ref[...] = jnp.where(lane == 0, e1, jnp.where(lane == 1, e2,
                            jnp.where(lane == 2, rank1, jnp.where(lane == 3, rank2, 0))))
    rf_ref[...] = jnp.where(lane == 0, g1, jnp.where(lane == 1, g2, 0.0))
    cnt_ref[...] = jnp.broadcast_to(c1 + c2, cnt_ref.shape).astype(jnp.int32)


def _post_mixer(mix_inputs, x, mem_kv, w_out, ln_g, ln_b, w_route, b_route, seq, ml):
    n = x.shape[0]
    tm = POST_TM
    tiles_per_batch = seq // tm
    row = lambda i: (i, 0)
    const = lambda i: (0, 0)
    if ml:
        h_f, h_b, proj, norm_g = mix_inputs
        ins = [h_f, h_b, proj, norm_g, proj]
        specs = [pl.BlockSpec((tm, ML_W_PAD), row),
                 pl.BlockSpec((tm, ML_W_PAD), row),
                 pl.BlockSpec((tm, ML_W_PAD), lambda i: (i, 3)),
                 pl.BlockSpec((1, ML_W_PAD), const),
                 pl.BlockSpec((tm, MEM_W), lambda i: (i, 4 * ML_W_PAD // MEM_W))]
    else:
        tok, proj = mix_inputs
        ins = [tok, proj]
        specs = [pl.BlockSpec((tm, TOKEN_W), row),
                 pl.BlockSpec((tm, MEM_W), lambda i: (i, 3 * TOKEN_W // MEM_W))]
    ins += [x, mem_kv, w_out, ln_g, ln_b, w_route, b_route]
    specs += [pl.BlockSpec((tm, D_MODEL), row),
              pl.BlockSpec((1, N_MEM, 2 * MEM_W), lambda i: (i // tiles_per_batch, 0, 0)),
              pl.BlockSpec(w_out.shape, const),
              pl.BlockSpec((1, D_MODEL), const),
              pl.BlockSpec((1, D_MODEL), const),
              pl.BlockSpec((D_MODEL, LANE), const),
              pl.BlockSpec((1, LANE), const)]
    return pl.pallas_call(
        functools.partial(_post_kernel, ml=ml),
        grid=(n // tm,),
        in_specs=specs,
        out_specs=[pl.BlockSpec((tm, D_MODEL), row),
                   pl.BlockSpec((tm, LANE), row),
                   pl.BlockSpec((tm, LANE), row),
                   pl.BlockSpec((8, LANE), row)],
        out_shape=[jax.ShapeDtypeStruct((n, D_MODEL), F32),
                   jax.ShapeDtypeStruct((n, LANE), jnp.int32),
                   jax.ShapeDtypeStruct((n, LANE), F32),
                   jax.ShapeDtypeStruct((n // tm * 8, LANE), jnp.int32)],
        compiler_params=_cparams("parallel"),
        name="post_mixer_ml" if ml else "post_mixer_na",
    )(*ins)


def _segment_copies(tile, dst_ref, cnt_ref, loff_ref, make_copy, wait):
    def seg(e, carry):
        idx = tile * N_EXPERTS + e
        n8 = cnt_ref[idx]
        l0 = loff_ref[idx]
        g0 = dst_ref[idx]
        for b in range(SEG_BITS):
            size = SEG_ALIGN << b
            done = n8 & (size - 1)

            @pl.when((n8 & size) != 0)
            def _():
                cp = make_copy(pl.multiple_of(l0 + done, SEG_ALIGN), pl.multiple_of(g0 + done, SEG_ALIGN), size)
                if wait:
                    cp.wait()
                else:
                    cp.start()
        return carry

    lax.fori_loop(0, N_EXPERTS, seg, 0)


def _local_positions(ri, lrow):
    lane = lax.broadcasted_iota(jnp.int32, ri.shape, 1)
    pos = []
    for k in range(TOP_K):
        off = jnp.sum(jnp.where(lane == ri[:, k:k + 1], lrow, 0).astype(F32), axis=-1, keepdims=True)
        pos.append(off.astype(jnp.int32) + ri[:, TOP_K + k:TOP_K + k + 1])
    return pos


def _dispatch_kernel(dst_ref, cnt_ref, loff_ref, zdst_ref, zcnt_ref, nused_ref,
                     x_ref, ri_ref, lrow_ref, xs_ref, sbuf, zbuf, sem):
    t = pl.program_id(0)
    tm = x_ref.shape[0]
    p0, p1 = _local_positions(ri_ref[...], lrow_ref[0, 0:1, :])
    lane = lax.broadcasted_iota(jnp.int32, (tm, LANE), 1)
    both = jnp.where(lane == 0, p0, jnp.where(lane == 1, p1, -1)).astype(F32)
    both_t = both.T.astype(jnp.int32)
    slot = lax.broadcasted_iota(jnp.int32, (LBUF, tm), 0)
    perm = jnp.where((slot == both_t[0:1, :]) | (slot == both_t[1:2, :]), 1.0, 0.0).astype(BF16)
    sbuf[...] = jnp.dot(perm, x_ref[...].astype(BF16), preferred_element_type=F32)

    def seg_copy(lrow, grow, n):
        return pltpu.make_async_copy(sbuf.at[pl.ds(lrow, n)], xs_ref.at[pl.ds(grow, n)], sem)

    _segment_copies(t, dst_ref, cnt_ref, loff_ref, seg_copy, wait=False)
    _segment_copies(t, dst_ref, cnt_ref, loff_ref, seg_copy, wait=True)

    @pl.when(t == pl.num_programs(0) - 1)
    def _():
        zbuf[...] = jnp.zeros_like(zbuf)
        for wait in (False, True):
            def fill(e, carry):
                n8 = zcnt_ref[e]
                g0 = zdst_ref[e]
                for b in range((MOE_BLK // SEG_ALIGN).bit_length() - 1):
                    size = SEG_ALIGN << b
                    done = n8 & (size - 1)

                    @pl.when((n8 & size) != 0)
                    def _():
                        cp = pltpu.make_async_copy(zbuf.at[pl.ds(0, size)],
                                                   xs_ref.at[pl.ds(pl.multiple_of(g0 + done, SEG_ALIGN), size)], sem)
                        if wait:
                            cp.wait()
                        else:
                            cp.start()
                return carry
            lax.fori_loop(0, N_EXPERTS, fill, 0)

            def fill_tail(h, carry):
                cp = pltpu.make_async_copy(zbuf, xs_ref.at[pl.ds(pl.multiple_of(h * zbuf.shape[0], SEG_ALIGN),
                                                                 zbuf.shape[0])], sem)
                if wait:
                    cp.wait()
                else:
                    cp.start()
                return carry
            halves = MOE_BLK // zbuf.shape[0]
            lax.fori_loop(nused_ref[0] * halves, xs_ref.shape[0] // zbuf.shape[0], fill_tail, 0)


def _dispatch(plan, x1, route_i, n_rows):
    n = x1.shape[0]
    tm = POST_TM
    row = lambda i, *_: (i, 0)
    return pl.pallas_call(
        _dispatch_kernel,
        grid_spec=pltpu.PrefetchScalarGridSpec(
            num_scalar_prefetch=6,
            grid=(n // tm,),
            in_specs=[pl.BlockSpec((tm, D_MODEL), row),
                      pl.BlockSpec((tm, LANE), row),
                      pl.BlockSpec((1, 8, LANE), lambda i, *_: (i, 0, 0)),
                      ],
            out_specs=pl.BlockSpec(memory_space=pl.ANY),
            scratch_shapes=[pltpu.VMEM((LBUF, D_MODEL), F32),
                            pltpu.VMEM((MOE_BLK // 2, D_MODEL), F32),
                            pltpu.SemaphoreType.DMA]),
        out_shape=jax.ShapeDtypeStruct((n_rows, D_MODEL), F32),
        compiler_params=_cparams("arbitrary"),
        name="moe_dispatch",
    )(plan["seg_dst"], plan["seg_cnt"], plan["seg_loff"], plan["fill_dst"], plan["fill_cnt"], plan["n_used"],
      x1, route_i, plan["loff_rows"])


def _expert_kernel(beid_ref, nused_ref, x_ref, w1_ref, w3_ref, w2_ref, o_ref, w1b, w3b, w2b):
    i = pl.program_id(0)
    prev = beid_ref[jnp.maximum(i - 1, 0)]
    fresh = (i == 0) | (beid_ref[i] != prev)

    @pl.when(fresh)
    def _():
        w1b[...] = w1_ref[...].astype(BF16)
        w3b[...] = w3_ref[...].astype(BF16)
        w2b[...] = w2_ref[...].astype(BF16)

    @pl.when(i < nused_ref[0])
    def _():
        x = x_ref[...].astype(BF16)
        a = jnp.dot(x, w1b[...], preferred_element_type=F32)
        b = jnp.dot(x, w3b[...], preferred_element_type=F32)
        hmid = (a * jax.nn.sigmoid(a) * b).astype(BF16)
        o_ref[...] = jnp.dot(hmid, w2b[...], preferred_element_type=F32)

    @pl.when(i >= nused_ref[0])
    def _():
        o_ref[...] = jnp.zeros_like(o_ref)


def _expert_mlp(block_eid, n_used, xs, w1, w3, w2, layer):
    p = xs.shape[0]
    nb = p // MOE_BLK
    wmap13 = lambda i, be, nu: (layer, be[i], 0, 0)
    return pl.pallas_call(
        _expert_kernel,
        grid_spec=pltpu.PrefetchScalarGridSpec(
            num_scalar_prefetch=2,
            grid=(nb,),
            in_specs=[pl.BlockSpec((MOE_BLK, D_MODEL), lambda i, be, nu: (jnp.minimum(i, nu[0] - 1), 0)),
                      pl.BlockSpec((None, None, D_MODEL, D_EXPERT), wmap13),
                      pl.BlockSpec((None, None, D_MODEL, D_EXPERT), wmap13),
                      pl.BlockSpec((None, None, D_EXPERT, D_MODEL), wmap13)],
            out_specs=pl.BlockSpec((MOE_BLK, D_MODEL), lambda i, be, nu: (i, 0)),
            scratch_shapes=[pltpu.VMEM((D_MODEL, D_EXPERT), BF16),
                            pltpu.VMEM((D_MODEL, D_EXPERT), BF16),
                            pltpu.VMEM((D_EXPERT, D_MODEL), BF16)]),
        out_shape=jax.ShapeDtypeStruct((p, D_MODEL), F32),
        compiler_params=_cparams("arbitrary"),
        name="moe_experts",
    )(block_eid, n_used, xs, w1, w3, w2)


def _combine_kernel(dst_ref, cnt_ref, loff_ref, x_ref, ri_ref, rf_ref, lrow_ref, g_ref, b_ref, ys_ref,
                    o_ref, ybuf, sem):
    t = pl.program_id(0)

    @pl.when(t == 0)
    def _():
        ybuf[...] = jnp.zeros_like(ybuf)

    def seg_copy(lrow, grow, n):
        return pltpu.make_async_copy(ys_ref.at[pl.ds(grow, n)], ybuf.at[pl.ds(lrow, n)], sem)

    _segment_copies(t, dst_ref, cnt_ref, loff_ref, seg_copy, wait=False)
    p0, p1 = _local_positions(ri_ref[...], lrow_ref[0, 0:1, :])
    slot = lax.broadcasted_iota(jnp.int32, (x_ref.shape[0], LBUF), 1)
    sel0 = jnp.where(slot == p0, 1.0, 0.0).astype(BF16)
    sel1 = jnp.where(slot == p1, 1.0, 0.0).astype(BF16)
    _segment_copies(t, dst_ref, cnt_ref, loff_ref, seg_copy, wait=True)

    yb = ybuf[...].astype(BF16)
    y0 = jnp.dot(sel0, yb, preferred_element_type=F32)
    y1 = jnp.dot(sel1, yb, preferred_element_type=F32)
    ffn = y0 * rf_ref[:, 0:1] + y1 * rf_ref[:, 1:2]
    o_ref[...] = _layer_norm(ALPHA * x_ref[...] + ffn, g_ref[...], b_ref[...])


def _combine_norm(plan, x1, route_i, route_f, ys, ln_g, ln_b):
    n = x1.shape[0]
    tm = POST_TM
    row = lambda i, *_: (i, 0)
    const = lambda i, *_: (0, 0)
    return pl.pallas_call(
        _combine_kernel,
        grid_spec=pltpu.PrefetchScalarGridSpec(
            num_scalar_prefetch=3,
            grid=(n // tm,),
            in_specs=[pl.BlockSpec((tm, D_MODEL), row),
                      pl.BlockSpec((tm, LANE), row),
                      pl.BlockSpec((tm, LANE), row),
                      pl.BlockSpec((1, 8, LANE), lambda i, *_: (i, 0, 0)),
                      pl.BlockSpec((1, D_MODEL), const),
                      pl.BlockSpec((1, D_MODEL), const),
                      pl.BlockSpec(memory_space=pl.ANY)],
            out_specs=pl.BlockSpec((tm, D_MODEL), row),
            scratch_shapes=[pltpu.VMEM((LBUF, D_MODEL), F32),
                            pltpu.SemaphoreType.DMA]),
        out_shape=jax.ShapeDtypeStruct((n, D_MODEL), F32),
        compiler_params=_cparams("arbitrary"),
        name="moe_combine_norm",
    )(plan["seg_dst"], plan["seg_cnt"], plan["seg_loff"], x1, route_i, route_f, plan["loff_rows"],
      ln_g, ln_b, ys)


def _routing_plan(tile_counts, n_tokens):
    n_tiles = tile_counts.shape[0]
    cnt8 = (tile_counts + SEG_ALIGN - 1) // SEG_ALIGN * SEG_ALIGN
    loff = jnp.cumsum(cnt8, axis=1) - cnt8
    tot8 = jnp.sum(cnt8, axis=0)
    padded = (tot8 + MOE_BLK - 1) // MOE_BLK * MOE_BLK
    pad_end = jnp.cumsum(padded)
    pad_start = pad_end - padded
    seg_dst = pad_start[None, :] + jnp.cumsum(cnt8, axis=0) - cnt8
    worst_rows = n_tokens * TOP_K + n_tiles * N_EXPERTS * (SEG_ALIGN - 1) + N_EXPERTS * (MOE_BLK - 1)
    n_blocks = -(-worst_rows // MOE_BLK)
    block_eid = jnp.minimum(
        jnp.searchsorted(pad_end, jnp.arange(n_blocks, dtype=jnp.int32) * MOE_BLK, side='right'),
        N_EXPERTS - 1).astype(jnp.int32)
    i32 = lambda v: v.astype(jnp.int32)
    loff_rows = jnp.broadcast_to(jnp.pad(i32(loff), ((0, 0), (0, LANE - N_EXPERTS)))[:, None, :],
                                 (n_tiles, 8, LANE))
    return dict(seg_dst=i32(seg_dst).reshape(-1), seg_cnt=i32(cnt8).reshape(-1), seg_loff=i32(loff).reshape(-1),
                fill_dst=i32(pad_start + tot8), fill_cnt=i32(padded - tot8), loff_rows=loff_rows,
                block_eid=block_eid, n_used=i32(pad_end[-1] // MOE_BLK).reshape(1), n_rows=n_blocks * MOE_BLK)


def _moe(x1, route_i, route_f, tile_cnt, w1, w3, w2, ln_g, ln_b, layer):
    n = x1.shape[0]
    counts = tile_cnt.reshape(n // POST_TM, 8, LANE)[:, 0, :N_EXPERTS]
    plan = _routing_plan(counts, n)
    xs = _dispatch(plan, x1, route_i, plan["n_rows"])
    ys = _expert_mlp(plan["block_eid"], plan["n_used"], xs, w1, w3, w2, layer)
    return _combine_norm(plan, x1, route_i, route_f, ys, ln_g, ln_b)


def _pad_heads(w):
    lead = w.shape[:-1]
    w = w.reshape(*lead, ML_HEADS, ML_DH)
    w = jnp.pad(w, [(0, 0)] * len(lead) + [(0, 0), (0, ML_DH_PAD - ML_DH)])
    return w.reshape(*lead, ML_W_PAD)


def _ml_in_weight(w):
    tw = TOKEN_W
    segs = [_pad_heads(w[:, i * tw:(i + 1) * tw]) for i in range(4)]
    gates = w[:, 4 * tw:4 * tw + 4 * ML_HEADS]
    qmem = w[:, 4 * tw + 4 * ML_HEADS:]
    gates = jnp.pad(gates, ((0, 0), (0, LANE - 4 * ML_HEADS)))
    return jnp.concatenate(segs + [qmem, gates], axis=1)


def _route_weight(wg, bg, we, be):
    w = jnp.concatenate([wg, we], axis=1)
    b = jnp.concatenate([bg, be], axis=0)
    pad = LANE - w.shape[1]
    return jnp.pad(w, ((0, 0), (0, pad))).astype(F32), jnp.pad(b, (0, pad)).reshape(1, LANE).astype(F32)


def kernel(x, mem, w_mem_kv, na_w_in, na_rpb, ml_w_in, ml_conv_w, ml_conv_b, ml_gate_b, ml_norm_g, w_out,
           ln1_g, ln1_b, ln2_g, ln2_b, moe_wg, moe_bg, moe_we, moe_be, moe_w1, moe_w3, moe_w2):
    batch, seq, d = x.shape
    n = batch * seq
    rows = seq // GRID_W
    xf = x.reshape(n, d)

    mem_kv = _matmul(mem.reshape(batch * N_MEM, d), w_mem_kv.astype(BF16), BF16,
                     min(PROJ_TM, batch * N_MEM), 2 * MEM_W, "mem_kv").reshape(batch, N_MEM, 2 * MEM_W)

    for layer in range(DEPTH):
        j = layer // 2
        w_route, b_route = _route_weight(moe_wg[layer], moe_bg[layer], moe_we[layer], moe_be[layer])
        g1 = ln1_g[layer].reshape(1, d)
        b1 = ln1_b[layer].reshape(1, d)
        if layer % 2 == 0:
            proj = _matmul(xf, na_w_in[j].astype(BF16), BF16, PROJ_TM, PROJ_TN, "na_in_proj")
            tok = _neighbourhood_attention(proj.reshape(batch, seq, -1), _na_bias_table(na_rpb[j], rows),
                                           batch, seq)
            x1, route_i, route_f, tile_cnt = _post_mixer((tok.reshape(n, TOKEN_W), proj), xf, mem_kv,
                                               w_out[layer].astype(BF16), g1, b1, w_route, b_route, seq, ml=False)
        else:
            proj = _matmul(xf, _ml_in_weight(ml_w_in[j]).astype(BF16), F32, PROJ_TM, PROJ_TN, "ml_in_proj")
            proj3 = proj.reshape(batch, seq, -1)
            conv_w = jnp.concatenate([_pad_heads(ml_conv_w[j][:, :TOKEN_W]),
                                      _pad_heads(ml_conv_w[j][:, TOKEN_W:])], axis=1)
            conv_b = jnp.concatenate([_pad_heads(ml_conv_b[j][:TOKEN_W]),
                                      _pad_heads(ml_conv_b[j][TOKEN_W:])]).reshape(1, 2 * ML_W_PAD)
            col_scale = jnp.concatenate([jnp.full((ML_W_PAD,), ML_DH ** -0.5, F32),
                                         jnp.ones((ML_W_PAD,), F32)]).reshape(1, 2 * ML_W_PAD)
            qk = _conv_silu(proj3, conv_w, conv_b, col_scale, batch, seq)
            gate_b = jnp.pad(ml_gate_b[j].reshape(-1), (0, LANE - 4 * ML_HEADS)).reshape(1, LANE)
            h_f = _mlstm_scan(qk, proj3, gate_b, batch, seq, reverse=False)
            h_b = _mlstm_scan(qk, proj3, gate_b, batch, seq, reverse=True)
            norm_g = _pad_heads(ml_norm_g[j]).reshape(1, ML_W_PAD)
            wo = jnp.concatenate([_pad_heads(w_out[layer][:TOKEN_W].T).T, w_out[layer][TOKEN_W:]], axis=0)
            x1, route_i, route_f, tile_cnt = _post_mixer((h_f.reshape(n, ML_W_PAD), h_b.reshape(n, ML_W_PAD), proj, norm_g),
                                               xf, mem_kv, wo.astype(BF16), g1, b1, w_route, b_route, seq, ml=True)
        xf = _moe(x1, route_i, route_f, tile_cnt, moe_w1, moe_w3, moe_w2,
                  ln2_g[layer].reshape(1, d), ln2_b[layer].reshape(1, d), layer)
    return xf.reshape(batch, seq, d)
```

```python
import functools
import math

import jax
import jax.numpy as jnp
from jax import lax
from jax.experimental import pallas as pl
from jax.experimental.pallas import tpu as pltpu

D_MODEL = 1024
DEPTH = 4
N_MEM = 256
GRID_W = 64
TOKEN_W = 768
MEM_W = 256
HEAD_DIM = 64
NA_HEADS = 12
WIN_H = 8
WIN_W = 16
ML_HEADS = 4
ML_DH = 192
ML_CHUNK = 64
CONV_K = 5
MEM_HEADS = 4
MEM_DH = 64
N_GROUPS = 4
EXPERTS_PER_GROUP = 8
N_EXPERTS = 32
TOP_K = 2
D_EXPERT = 512
LN_EPS = 1e-5
ALPHA = (2 * DEPTH) ** 0.25

LANE = 128
ML_DH_PAD = 256
ML_W_PAD = ML_HEADS * ML_DH_PAD
VMEM_LIMIT_BYTES = 56 * 1024 * 1024

PROJ_TM = 512
PROJ_TN = 640
NA_ROW_UNROLL = 4
POST_TM = 512
MOE_BLK = 256
SEG_ALIGN = 8
SEG_BITS = (TOP_K * POST_TM // SEG_ALIGN).bit_length()
LBUF = -(-(TOP_K * POST_TM + N_EXPERTS * (SEG_ALIGN - 1)) // 256) * 256
NEG_BIG = -1e30

F32 = jnp.float32
BF16 = jnp.bfloat16


def _cparams(*sem):
    return pltpu.CompilerParams(dimension_semantics=sem, vmem_limit_bytes=VMEM_LIMIT_BYTES)


def _matmul_kernel(x_ref, w_ref, o_ref, *, tn):
    x = x_ref[...].astype(BF16)
    for j in range(w_ref.shape[1] // tn):
        cols = slice(j * tn, (j + 1) * tn)
        o_ref[:, cols] = jnp.dot(x, w_ref[:, cols], preferred_element_type=F32).astype(o_ref.dtype)


def _matmul(x, w, out_dtype, tm, tn, name):
    m, k = x.shape
    n = w.shape[1]
    assert m % tm == 0 and n % tn == 0
    return pl.pallas_call(
        functools.partial(_matmul_kernel, tn=tn),
        grid=(m // tm,),
        in_specs=[pl.BlockSpec((tm, k), lambda i: (i, 0)),
                  pl.BlockSpec((k, n), lambda i: (0, 0))],
        out_specs=pl.BlockSpec((tm, n), lambda i: (i, 0)),
        out_shape=jax.ShapeDtypeStruct((m, n), out_dtype),
        compiler_params=_cparams("parallel"),
        name=name,
    )(x, w)


def _na_bias_table(rpb, rows):
    kh = min(WIN_H, rows)
    off = jnp.arange(kh)[:, None]
    j = jnp.arange(kh)[None, :]
    ri = j - off + (WIN_H - 1)
    c = jnp.arange(GRID_W)[:, None]
    kc = jnp.arange(GRID_W)[None, :]
    cs = jnp.clip(c - WIN_W // 2, 0, GRID_W - WIN_W)
    valid = (kc >= cs) & (kc < cs + WIN_W)
    ci = kc - c + (WIN_W - 1)
    row_sel = (ri[:, :, None] == jnp.arange(2 * WIN_H - 1)).astype(F32)
    col_sel = (ci[:, :, None] == jnp.arange(2 * WIN_W - 1)).astype(F32)
    tab = jnp.einsum('hab,oja,ckb->hocjk', rpb.astype(F32), row_sel, col_sel,
                     precision=lax.Precision.HIGHEST)
    tab = jnp.where(valid[None, None, :, None, :], tab, NEG_BIG)
    tab = tab.reshape(NA_HEADS // 2, 2, kh, GRID_W, kh * GRID_W)
    tab = tab.transpose(0, 2, 1, 3, 4).reshape(NA_HEADS // 2, kh, 2 * GRID_W, kh * GRID_W)
    return tab.astype(F32)


def _na_kernel(q_ref, k_ref, v_ref, b_ref, o_ref, *, rows, kh):
    lane = lax.broadcasted_iota(jnp.int32, (GRID_W, LANE), 1)
    first = lane < HEAD_DIM
    scale = HEAD_DIM ** -0.5

    def body(r, carry):
        rs = jnp.clip(r - kh // 2, 0, rows - kh)
        off = r - rs
        q = q_ref[0, pl.ds(pl.multiple_of(r * GRID_W, GRID_W), GRID_W), :]
        kb = k_ref[0, pl.ds(pl.multiple_of(rs * GRID_W, GRID_W), kh * GRID_W), :]
        vb = v_ref[0, pl.ds(pl.multiple_of(rs * GRID_W, GRID_W), kh * GRID_W), :]
        zero = jnp.zeros_like(q)
        qq = jnp.concatenate([jnp.where(first, q, zero), jnp.where(first, zero, q)], axis=0)
        s = lax.dot_general(qq, kb, (((1,), (1,)), ((), ())), preferred_element_type=F32)
        s = s * scale + b_ref[0, off]
        m = jnp.max(s, axis=-1, keepdims=True)
        p = jnp.exp(s - m)
        l = jnp.sum(p, axis=-1, keepdims=True)
        o = jnp.dot(p.astype(BF16), vb, preferred_element_type=F32) / l
        out = jnp.where(first, o[:GRID_W], o[GRID_W:])
        o_ref[0, pl.ds(pl.multiple_of(r * GRID_W, GRID_W), GRID_W), :] = out.astype(o_ref.dtype)
        return carry

    lax.fori_loop(0, rows, body, 0, unroll=NA_ROW_UNROLL)


def _neighbourhood_attention(proj, bias_tab, batch, seq):
    rows = seq // GRID_W
    kh = min(WIN_H, rows)
    npair = NA_HEADS // 2
    kern = functools.partial(_na_kernel, rows=rows, kh=kh)
    return pl.pallas_call(
        kern,
        grid=(batch, npair),
        in_specs=[pl.BlockSpec((1, seq, LANE), lambda b, h: (b, 0, h)),
                  pl.BlockSpec((1, seq, LANE), lambda b, h: (b, 0, npair + h)),
                  pl.BlockSpec((1, seq, LANE), lambda b, h: (b, 0, 2 * npair + h)),
                  pl.BlockSpec((1, kh, 2 * GRID_W, kh * GRID_W), lambda b, h: (h, 0, 0, 0))],
        out_specs=pl.BlockSpec((1, seq, LANE), lambda b, h: (b, 0, h)),
        out_shape=jax.ShapeDtypeStruct((batch, seq, TOKEN_W), BF16),
        compiler_params=_cparams("parallel", "arbitrary"),
        name="na_attention",
    )(proj, proj, proj, bias_tab)


def _conv_silu_kernel(x_ref, w_ref, b_ref, s_ref, o_ref, *, seq):
    x = x_ref[0]
    t = lax.broadcasted_iota(jnp.int32, x.shape, 0)
    acc = jnp.zeros_like(x) + b_ref[...]
    for j in range(CONV_K):
        sh = CONV_K // 2 - j
        if sh == 0:
            xs = x
        elif sh > 0:
            xs = jnp.where(t >= sh, pltpu.roll(x, sh, 0), 0.0)
        else:
            xs = jnp.where(t < seq + sh, pltpu.roll(x, seq + sh, 0), 0.0)
        acc = acc + w_ref[j:j + 1, :] * xs
    y = acc * jax.nn.sigmoid(acc) * s_ref[...]
    o_ref[0] = y.astype(o_ref.dtype)


def _conv_silu(proj, conv_w, conv_b, col_scale, batch, seq):
    ncol = 2 * ML_W_PAD // LANE
    kern = functools.partial(_conv_silu_kernel, seq=seq)
    return pl.pallas_call(
        kern,
        grid=(batch, ncol),
        in_specs=[pl.BlockSpec((1, seq, LANE), lambda b, j: (b, 0, j)),
                  pl.BlockSpec((CONV_K, LANE), lambda b, j: (0, j)),
                  pl.BlockSpec((1, LANE), lambda b, j: (0, j)),
                  pl.BlockSpec((1, LANE), lambda b, j: (0, j))],
        out_specs=pl.BlockSpec((1, seq, LANE), lambda b, j: (b, 0, j)),
        out_shape=jax.ShapeDtypeStruct((batch, seq, 2 * ML_W_PAD), BF16),
        compiler_params=_cparams("parallel", "arbitrary"),
        name="ml_conv_silu",
    )(proj, conv_w, conv_b, col_scale)


def _log_sigmoid(x):
    return jnp.minimum(x, 0.0) - jnp.log1p(jnp.exp(-jnp.abs(x)))


def _mlstm_chunk(q_ref, k_ref, v_ref, g_ref, gb_ref, o_ref, c_ref, m_ref, reverse):
    L = ML_CHUNK
    d = 1 if reverse else 0
    g = g_ref[0] + gb_ref[...]
    glane = lax.broadcasted_iota(jnp.int32, g.shape, 1)
    is_f = (glane >= 2 * ML_HEADS) & (glane < 4 * ML_HEADS)
    act = jnp.where(is_f, _log_sigmoid(g), g)
    row = lax.broadcasted_iota(jnp.int32, (L, L), 0)
    col = lax.broadcasted_iota(jnp.int32, (L, L), 1)
    valid = (col >= row) if reverse else (col <= row)
    bcum = jnp.dot(valid.astype(F32), act, preferred_element_type=F32,
                   precision=lax.Precision.HIGHEST)
    act_t = act.T
    bcum_t = bcum.T
    last = 0 if reverse else L - 1
    hlane = lax.broadcasted_iota(jnp.int32, (L, ML_DH_PAD), 1)

    for h in range(ML_HEADS):
        li_lane = d * ML_HEADS + h
        lf_lane = 2 * ML_HEADS + d * ML_HEADS + h
        b_col = bcum[:, lf_lane:lf_lane + 1]
        b_row = bcum_t[lf_lane:lf_lane + 1, :]
        li_col = act[:, li_lane:li_lane + 1]
        li_row = act_t[li_lane:li_lane + 1, :]
        dm = jnp.where(valid, b_col - b_row + li_row, NEG_BIG)
        m_prev = m_ref[h][0:1, 0:1]
        inter = b_col + m_prev
        m_t = jnp.maximum(inter, jnp.max(dm, axis=-1, keepdims=True))
        w_inter = jnp.exp(inter - m_t)

        sl = slice(h * ML_DH_PAD, (h + 1) * ML_DH_PAD)
        qh = q_ref[0, :, sl]
        kh = k_ref[0, :, sl]
        vh = jnp.where(hlane == ML_DH, 1.0, v_ref[0, :, sl]).astype(BF16)
        s = lax.dot_general(qh, kh, (((1,), (1,)), ((), ())), preferred_element_type=F32)
        s = s * jnp.exp(dm - m_t)
        c_old = c_ref[h]
        num = (w_inter * jnp.dot(qh, c_old.astype(BF16), preferred_element_type=F32)
               + jnp.dot(s.astype(BF16), vh, preferred_element_type=F32))
        den = num[:, ML_DH:ML_DH + 1]
        hout = num / jnp.maximum(jnp.abs(den), jnp.exp(-m_t))
        o_ref[0, :, sl] = jnp.where(hlane < ML_DH, hout, 0.0)

        b_last = b_col[last:last + 1, :]
        m_new = m_t[last:last + 1, :]
        decay = jnp.exp(b_last + m_prev - m_new)
        w_src = jnp.exp(b_last - b_col + li_col - m_new)
        kw = (kh.astype(F32) * w_src).astype(BF16)
        upd = lax.dot_general(kw, vh, (((0,), (0,)), ((), ())), preferred_element_type=F32)
        c_ref[h] = decay * c_old + upd
        m_ref[h] = jnp.zeros(m_ref.shape[1:], F32) + m_new


def _mlstm_kernel(qf_ref, kf_ref, vf_ref, gf_ref, qb_ref, kb_ref, vb_ref, gr_ref, gb_ref,
                  of_ref, ob_ref, c_ref, m_ref):
    @pl.when(pl.program_id(1) == 0)
    def _():
        c_ref[...] = jnp.zeros_like(c_ref)
        m_ref[...] = jnp.zeros_like(m_ref)

    _mlstm_chunk(qf_ref, kf_ref, vf_ref, gf_ref, gb_ref, of_ref, c_ref.at[0], m_ref.at[0], reverse=False)
    _mlstm_chunk(qb_ref, kb_ref, vb_ref, gr_ref, gb_ref, ob_ref, c_ref.at[1], m_ref.at[1], reverse=True)


def _mlstm_scan(qk, proj, gate_b, batch, seq):
    nc = seq // ML_CHUNK
    gcol = (4 * ML_W_PAD + MEM_W) // LANE
    fwd = lambda col: (lambda b, c: (b, c, col))
    bwd = lambda col: (lambda b, c: (b, nc - 1 - c, col))
    wide = lambda m: pl.BlockSpec((1, ML_CHUNK, ML_W_PAD), m)
    gate = lambda m: pl.BlockSpec((1, ML_CHUNK, LANE), m)
    out = jax.ShapeDtypeStruct((batch, seq, ML_W_PAD), F32)
    return pl.pallas_call(
        _mlstm_kernel,
        grid=(batch, nc),
        in_specs=[wide(fwd(0)), wide(fwd(1)), wide(fwd(2)), gate(fwd(gcol)),
                  wide(bwd(0)), wide(bwd(1)), wide(bwd(2)), gate(bwd(gcol)),
                  pl.BlockSpec((1, LANE), lambda b, c: (0, 0))],
        out_specs=[wide(fwd(0)), wide(bwd(0))],
        out_shape=[out, out],
        scratch_shapes=[pltpu.VMEM((2, ML_HEADS, ML_DH_PAD, ML_DH_PAD), F32),
                        pltpu.VMEM((2, ML_HEADS, 8, LANE), F32)],
        compiler_params=_cparams("parallel", "arbitrary"),
        name="mlstm_scan",
    )(qk, qk, proj, proj, qk, qk, proj, proj, gate_b)


def _layer_norm(y, g, b):
    mu = jnp.mean(y, axis=-1, keepdims=True)
    yc = y - mu
    var = jnp.mean(yc * yc, axis=-1, keepdims=True)
    return yc * lax.rsqrt(var + LN_EPS) * g + b


def _first_argmax(val, lane):
    mx = jnp.max(val, axis=-1, keepdims=True)
    idx = jnp.min(jnp.where(val == mx, lane, LANE), axis=-1, keepdims=True)
    return mx, idx


def _route(x1, wr_ref, br_ref):
    logits = jnp.dot(x1, wr_ref[...], preferred_element_type=F32,
                     precision=lax.Precision.HIGHEST) + br_ref[...]
    lane = lax.broadcasted_iota(jnp.int32, logits.shape, 1)
    is_g = lane < N_GROUPS
    gl = jnp.where(is_g, logits, NEG_BIG)
    gmax, gidx = _first_argmax(gl, lane)
    gsum = jnp.sum(jnp.where(is_g, jnp.exp(gl - gmax), 0.0), axis=-1, keepdims=True)
    g_gate = 1.0 / gsum
    lo = N_GROUPS + gidx * EXPERTS_PER_GROUP
    in_grp = (lane >= lo) & (lane < lo + EXPERTS_PER_GROUP)
    el = jnp.where(in_grp, logits, NEG_BIG)
    emax = jnp.max(el, axis=-1, keepdims=True)
    pe = jnp.where(in_grp, jnp.exp(el - emax), 0.0)
    prob = pe / jnp.sum(pe, axis=-1, keepdims=True)
    prob = jnp.where(in_grp, prob, -1.0)
    p1, i1 = _first_argmax(prob, lane)
    p2, i2 = _first_argmax(jnp.where(lane == i1, -1.0, prob), lane)
    tot = p1 + p2
    return i1 - N_GROUPS, i2 - N_GROUPS, g_gate * p1 / tot, g_gate * p2 / tot


def _post_kernel(*refs, ml):
    if ml:
        (hf_ref, hb_ref, op_ref, ng_ref, qm_ref, x_ref, mkv_ref, wo_ref, g_ref, b_ref, wr_ref, br_ref,
         x1_ref, ri_ref, rf_ref, cnt_ref) = refs
        hsum = hf_ref[...] + hb_ref[...]
        hl = lax.broadcasted_iota(jnp.int32, (hsum.shape[0], ML_DH_PAD), 1)
        parts = []
        for h in range(ML_HEADS):
            hh = hsum[:, h * ML_DH_PAD:(h + 1) * ML_DH_PAD]
            mu = jnp.sum(hh, axis=-1, keepdims=True) * (1.0 / ML_DH)
            hc = jnp.where(hl < ML_DH, hh - mu, 0.0)
            var = jnp.sum(hc * hc, axis=-1, keepdims=True) * (1.0 / ML_DH)
            parts.append(hc * lax.rsqrt(var + LN_EPS))
        hn = jnp.concatenate(parts, axis=-1) * ng_ref[...]
        tok = (jax.nn.sigmoid(op_ref[...]) * hn).astype(BF16)
    else:
        (tok_ref, qm_ref, x_ref, mkv_ref, wo_ref, g_ref, b_ref, wr_ref, br_ref,
         x1_ref, ri_ref, rf_ref, cnt_ref) = refs
        tok = tok_ref[...]

    qm = qm_ref[...].astype(BF16)
    mk = mkv_ref[0, :, 0:MEM_W]
    mv = mkv_ref[0, :, MEM_W:2 * MEM_W]
    mlane = lax.broadcasted_iota(jnp.int32, qm.shape, 1)
    matt = jnp.zeros(qm.shape, F32)
    for h in range(MEM_HEADS):
        hm = (mlane >= h * MEM_DH) & (mlane < (h + 1) * MEM_DH)
        qh = jnp.where(hm, qm, jnp.zeros_like(qm))
        s = lax.dot_general(qh, mk, (((1,), (1,)), ((), ())), preferred_element_type=F32) * (MEM_DH ** -0.5)
        p = jnp.exp(s - jnp.max(s, axis=-1, keepdims=True))
        l = jnp.sum(p, axis=-1, keepdims=True)
        o = jnp.dot(p.astype(BF16), mv, preferred_element_type=F32) / l
        matt = jnp.where(hm, o, matt)

    mixed = jnp.concatenate([tok, matt.astype(BF16)], axis=-1)
    y = ALPHA * x_ref[...] + jnp.dot(mixed, wo_ref[...], preferred_element_type=F32)
    x1 = _layer_norm(y, g_ref[...], b_ref[...])
    x1_ref[...] = x1

    e1, e2, g1, g2 = _route(x1, wr_ref, br_ref)
    lane = lax.broadcasted_iota(jnp.int32, ri_ref.shape, 1)
    tm = x1.shape[0]
    oh1 = jnp.where(lane == e1, 1.0, 0.0)
    oh2 = jnp.where(lane == e2, 1.0, 0.0)
    trow = lax.broadcasted_iota(jnp.int32, (tm, tm), 0)
    tcol = lax.broadcasted_iota(jnp.int32, (tm, tm), 1)
    earlier = jnp.where(tcol < trow, 1.0, 0.0).astype(BF16)
    before1 = jnp.dot(earlier, oh1.astype(BF16), preferred_element_type=F32)
    before2 = jnp.dot(earlier, oh2.astype(BF16), preferred_element_type=F32)
    c1 = jnp.sum(oh1, axis=0, keepdims=True)
    c2 = jnp.sum(oh2, axis=0, keepdims=True)
    rank1 = jnp.sum(oh1 * before1, axis=-1, keepdims=True).astype(jnp.int32)
    rank2 = jnp.sum(oh2 * (before2 + c1), axis=-1, keepdims=True).astype(jnp.int32)
    ri_ref[...] = jnp.where(lane == 0, e1, jnp.where(lane == 1, e2,
                            jnp.where(lane == 2, rank1, jnp.where(lane == 3, rank2, 0))))
    rf_ref[...] = jnp.where(lane == 0, g1, jnp.where(lane == 1, g2, 0.0))
    cnt_ref[...] = jnp.broadcast_to(c1 + c2, cnt_ref.shape).astype(jnp.int32)


def _post_mixer(mix_inputs, x, mem_kv, w_out, ln_g, ln_b, w_route, b_route, seq, ml):
    n = x.shape[0]
    tm = POST_TM
    tiles_per_batch = seq // tm
    row = lambda i: (i, 0)
    const = lambda i: (0, 0)
    if ml:
        h_f, h_b, proj, norm_g = mix_inputs
        ins = [h_f, h_b, proj, norm_g, proj]
        specs = [pl.BlockSpec((tm, ML_W_PAD), row),
                 pl.BlockSpec((tm, ML_W_PAD), row),
                 pl.BlockSpec((tm, ML_W_PAD), lambda i: (i, 3)),
                 pl.BlockSpec((1, ML_W_PAD), const),
                 pl.BlockSpec((tm, MEM_W), lambda i: (i, 4 * ML_W_PAD // MEM_W))]
    else:
        tok, proj = mix_inputs
        ins = [tok, proj]
        specs = [pl.BlockSpec((tm, TOKEN_W), row),
                 pl.BlockSpec((tm, MEM_W), lambda i: (i, 3 * TOKEN_W // MEM_W))]
    ins += [x, mem_kv, w_out, ln_g, ln_b, w_route, b_route]
    specs += [pl.BlockSpec((tm, D_MODEL), row),
              pl.BlockSpec((1, N_MEM, 2 * MEM_W), lambda i: (i // tiles_per_batch, 0, 0)),
              pl.BlockSpec(w_out.shape, const),
              pl.BlockSpec((1, D_MODEL), const),
              pl.BlockSpec((1, D_MODEL), const),
              pl.BlockSpec((D_MODEL, LANE), const),
              pl.BlockSpec((1, LANE), const)]
    return pl.pallas_call(
        functools.partial(_post_kernel, ml=ml),
        grid=(n // tm,),
        in_specs=specs,
        out_specs=[pl.BlockSpec((tm, D_MODEL), row),
                   pl.BlockSpec((tm, LANE), row),
                   pl.BlockSpec((tm, LANE), row),
                   pl.BlockSpec((8, LANE), row)],
        out_shape=[jax.ShapeDtypeStruct((n, D_MODEL), F32),
                   jax.ShapeDtypeStruct((n, LANE), jnp.int32),
                   jax.ShapeDtypeStruct((n, LANE), F32),
                   jax.ShapeDtypeStruct((n // tm * 8, LANE), jnp.int32)],
        compiler_params=_cparams("parallel"),
        name="post_mixer_ml" if ml else "post_mixer_na",
    )(*ins)


def _segment_copies(tile, dst_ref, cnt_ref, loff_ref, make_copy, wait):
    def seg(e, carry):
        idx = tile * N_EXPERTS + e
        n8 = cnt_ref[idx]
        l0 = loff_ref[idx]
        g0 = dst_ref[idx]
        for b in range(SEG_BITS):
            size = SEG_ALIGN << b
            done = n8 & (size - 1)

            @pl.when((n8 & size) != 0)
            def _():
                cp = make_copy(pl.multiple_of(l0 + done, SEG_ALIGN), pl.multiple_of(g0 + done, SEG_ALIGN), size)
                if wait:
                    cp.wait()
                else:
                    cp.start()
        return carry

    lax.fori_loop(0, N_EXPERTS, seg, 0)


def _local_positions(ri, lrow):
    lane = lax.broadcasted_iota(jnp.int32, ri.shape, 1)
    pos = []
    for k in range(TOP_K):
        off = jnp.sum(jnp.where(lane == ri[:, k:k + 1], lrow, 0).astype(F32), axis=-1, keepdims=True)
        pos.append(off.astype(jnp.int32) + ri[:, TOP_K + k:TOP_K + k + 1])
    return pos


def _dispatch_kernel(dst_ref, cnt_ref, loff_ref, zdst_ref, zcnt_ref, nused_ref,
                     x_ref, ri_ref, lrow_ref, xs_ref, sbuf, zbuf, sems):
    t = pl.program_id(0)
    cur = lax.rem(t, 2)
    tm = x_ref.shape[0]
    p0, p1 = _local_positions(ri_ref[...], lrow_ref[0, 0:1, :])
    lane = lax.broadcasted_iota(jnp.int32, (tm, LANE), 1)
    both = jnp.where(lane == 0, p0, jnp.where(lane == 1, p1, -1)).astype(F32)
    both_t = both.T.astype(jnp.int32)
    row = lax.broadcasted_iota(jnp.int32, (LBUF, tm), 0)
    perm = jnp.where((row == both_t[0:1, :]) | (row == both_t[1:2, :]), 1.0, 0.0).astype(BF16)
    sbuf[cur] = jnp.dot(perm, x_ref[...].astype(BF16), preferred_element_type=F32)

    def seg_copy(slot):
        def make(lrow, grow, n):
            return pltpu.make_async_copy(sbuf.at[slot, pl.ds(lrow, n)], xs_ref.at[pl.ds(grow, n)], sems.at[slot])
        return make

    _segment_copies(t, dst_ref, cnt_ref, loff_ref, seg_copy(cur), wait=False)

    @pl.when(t > 0)
    def _():
        _segment_copies(t - 1, dst_ref, cnt_ref, loff_ref, seg_copy(1 - cur), wait=True)

    @pl.when(t == pl.num_programs(0) - 1)
    def _():
        _segment_copies(t, dst_ref, cnt_ref, loff_ref, seg_copy(cur), wait=True)
        sem = sems.at[0]
        zbuf[...] = jnp.zeros_like(zbuf)
        for wait in (False, True):
            def fill(e, carry):
                n8 = zcnt_ref[e]
                g0 = zdst_ref[e]
                for b in range((MOE_BLK // SEG_ALIGN).bit_length() - 1):
                    size = SEG_ALIGN << b
                    done = n8 & (size - 1)

                    @pl.when((n8 & size) != 0)
                    def _():
                        cp = pltpu.make_async_copy(zbuf.at[pl.ds(0, size)],
                                                   xs_ref.at[pl.ds(pl.multiple_of(g0 + done, SEG_ALIGN), size)], sem)
                        if wait:
                            cp.wait()
                        else:
                            cp.start()
                return carry
            lax.fori_loop(0, N_EXPERTS, fill, 0)

            def fill_tail(h, carry):
                cp = pltpu.make_async_copy(zbuf, xs_ref.at[pl.ds(pl.multiple_of(h * zbuf.shape[0], SEG_ALIGN),
                                                                 zbuf.shape[0])], sem)
                if wait:
                    cp.wait()
                else:
                    cp.start()
                return carry
            halves = MOE_BLK // zbuf.shape[0]
            lax.fori_loop(nused_ref[0] * halves, xs_ref.shape[0] // zbuf.shape[0], fill_tail, 0)


def _dispatch(plan, x1, route_i, n_rows):
    n = x1.shape[0]
    tm = POST_TM
    row = lambda i, *_: (i, 0)
    return pl.pallas_call(
        _dispatch_kernel,
        grid_spec=pltpu.PrefetchScalarGridSpec(
            num_scalar_prefetch=6,
            grid=(n // tm,),
            in_specs=[pl.BlockSpec((tm, D_MODEL), row),
                      pl.BlockSpec((tm, LANE), row),
                      pl.BlockSpec((1, 8, LANE), lambda i, *_: (i, 0, 0)),
                      ],
            out_specs=pl.BlockSpec(memory_space=pl.ANY),
            scratch_shapes=[pltpu.VMEM((2, LBUF, D_MODEL), F32),
                            pltpu.VMEM((MOE_BLK // 2, D_MODEL), F32),
                            pltpu.SemaphoreType.DMA((2,))]),
        out_shape=jax.ShapeDtypeStruct((n_rows, D_MODEL), F32),
        compiler_params=_cparams("arbitrary"),
        name="moe_dispatch",
    )(plan["seg_dst"], plan["seg_cnt"], plan["seg_loff"], plan["fill_dst"], plan["fill_cnt"], plan["n_used"],
      x1, route_i, plan["loff_rows"])


def _expert_kernel(beid_ref, nused_ref, x_ref, w1_ref, w3_ref, w2_ref, o_ref, w1b, w3b, w2b):
    i = pl.program_id(0)
    prev = beid_ref[jnp.maximum(i - 1, 0)]
    fresh = (i == 0) | (beid_ref[i] != prev)

    @pl.when(fresh)
    def _():
        w1b[...] = w1_ref[...].astype(BF16)
        w3b[...] = w3_ref[...].astype(BF16)
        w2b[...] = w2_ref[...].astype(BF16)

    @pl.when(i < nused_ref[0])
    def _():
        x = x_ref[...].astype(BF16)
        a = jnp.dot(x, w1b[...], preferred_element_type=F32)
        b = jnp.dot(x, w3b[...], preferred_element_type=F32)
        hmid = (a * jax.nn.sigmoid(a) * b).astype(BF16)
        o_ref[...] = jnp.dot(hmid, w2b[...], preferred_element_type=F32)

    @pl.when(i >= nused_ref[0])
    def _():
        o_ref[...] = jnp.zeros_like(o_ref)


def _expert_mlp(block_eid, n_used, xs, w1, w3, w2, layer):
    p = xs.shape[0]
    nb = p // MOE_BLK
    wmap13 = lambda i, be, nu: (layer, be[i], 0, 0)
    return pl.pallas_call(
        _expert_kernel,
        grid_spec=pltpu.PrefetchScalarGridSpec(
            num_scalar_prefetch=2,
            grid=(nb,),
            in_specs=[pl.BlockSpec((MOE_BLK, D_MODEL), lambda i, be, nu: (jnp.maximum(jnp.minimum(i, nu[0] - 1), 0), 0)),
                      pl.BlockSpec((None, None, D_MODEL, D_EXPERT), wmap13),
                      pl.BlockSpec((None, None, D_MODEL, D_EXPERT), wmap13),
                      pl.BlockSpec((None, None, D_EXPERT, D_MODEL), wmap13)],
            out_specs=pl.BlockSpec((MOE_BLK, D_MODEL), lambda i, be, nu: (i, 0)),
            scratch_shapes=[pltpu.VMEM((D_MODEL, D_EXPERT), BF16),
                            pltpu.VMEM((D_MODEL, D_EXPERT), BF16),
                            pltpu.VMEM((D_EXPERT, D_MODEL), BF16)]),
        out_shape=jax.ShapeDtypeStruct((p, D_MODEL), F32),
        compiler_params=_cparams("arbitrary"),
        name="moe_experts",
    )(block_eid, n_used, xs, w1, w3, w2)


def _combine_kernel(dst_ref, cnt_ref, loff_ref, x_ref, ri_ref, rf_ref, lrow_ref, g_ref, b_ref, ys_ref,
                    o_ref, ybuf, sems):
    t = pl.program_id(0)
    cur = lax.rem(t, 2)

    def seg_copy(slot):
        def make(lrow, grow, n):
            return pltpu.make_async_copy(ys_ref.at[pl.ds(grow, n)], ybuf.at[slot, pl.ds(lrow, n)], sems.at[slot])
        return make

    @pl.when(t == 0)
    def _():
        ybuf[...] = jnp.zeros_like(ybuf)
        _segment_copies(t, dst_ref, cnt_ref, loff_ref, seg_copy(cur), wait=False)

    @pl.when(t + 1 < pl.num_programs(0))
    def _():
        _segment_copies(t + 1, dst_ref, cnt_ref, loff_ref, seg_copy(1 - cur), wait=False)

    p0, p1 = _local_positions(ri_ref[...], lrow_ref[0, 0:1, :])
    col = lax.broadcasted_iota(jnp.int32, (x_ref.shape[0], LBUF), 1)
    sel0 = jnp.where(col == p0, 1.0, 0.0).astype(BF16)
    sel1 = jnp.where(col == p1, 1.0, 0.0).astype(BF16)
    _segment_copies(t, dst_ref, cnt_ref, loff_ref, seg_copy(cur), wait=True)

    yb = ybuf[cur].astype(BF16)
    y0 = jnp.dot(sel0, yb, preferred_element_type=F32)
    y1 = jnp.dot(sel1, yb, preferred_element_type=F32)
    ffn = y0 * rf_ref[:, 0:1] + y1 * rf_ref[:, 1:2]
    o_ref[...] = _layer_norm(ALPHA * x_ref[...] + ffn, g_ref[...], b_ref[...])


def _combine_norm(plan, x1, route_i, route_f, ys, ln_g, ln_b):
    n = x1.shape[0]
    tm = POST_TM
    row = lambda i, *_: (i, 0)
    const = lambda i, *_: (0, 0)
    return pl.pallas_call(
        _combine_kernel,
        grid_spec=pltpu.PrefetchScalarGridSpec(
            num_scalar_prefetch=3,
            grid=(n // tm,),
            in_specs=[pl.BlockSpec((tm, D_MODEL), row),
                      pl.BlockSpec((tm, LANE), row),
                      pl.BlockSpec((tm, LANE), row),
                      pl.BlockSpec((1, 8, LANE), lambda i, *_: (i, 0, 0)),
                      pl.BlockSpec((1, D_MODEL), const),
                      pl.BlockSpec((1, D_MODEL), const),
                      pl.BlockSpec(memory_space=pl.ANY)],
            out_specs=pl.BlockSpec((tm, D_MODEL), row),
            scratch_shapes=[pltpu.VMEM((2, LBUF, D_MODEL), F32),
                            pltpu.SemaphoreType.DMA((2,))]),
        out_shape=jax.ShapeDtypeStruct((n, D_MODEL), F32),
        compiler_params=_cparams("arbitrary"),
        name="moe_combine_norm",
    )(plan["seg_dst"], plan["seg_cnt"], plan["seg_loff"], x1, route_i, route_f, plan["loff_rows"],
      ln_g, ln_b, ys)


def _routing_plan(tile_counts, n_tokens):
    n_tiles = tile_counts.shape[0]
    e_before = jnp.arange(N_EXPERTS)[:, None] < jnp.arange(N_EXPERTS)[None, :]
    t_before = jnp.arange(n_tiles)[:, None] < jnp.arange(n_tiles)[None, :]
    cnt8 = (tile_counts + SEG_ALIGN - 1) // SEG_ALIGN * SEG_ALIGN
    loff = jnp.sum(jnp.where(e_before[None], cnt8[:, :, None], 0), axis=1)
    tot8 = jnp.sum(cnt8, axis=0)
    padded = (tot8 + MOE_BLK - 1) // MOE_BLK * MOE_BLK
    pad_start = jnp.sum(jnp.where(e_before, padded[:, None], 0), axis=0)
    pad_end = pad_start + padded
    seg_dst = pad_start[None, :] + jnp.sum(jnp.where(t_before[:, :, None], cnt8[:, None, :], 0), axis=0)
    worst_rows = n_tokens * TOP_K + n_tiles * N_EXPERTS * (SEG_ALIGN - 1) + N_EXPERTS * (MOE_BLK - 1)
    n_blocks = -(-worst_rows // MOE_BLK)
    block_row = jnp.arange(n_blocks, dtype=jnp.int32) * MOE_BLK
    block_eid = jnp.minimum(jnp.sum(pad_end[None, :] <= block_row[:, None], axis=1), N_EXPERTS - 1).astype(jnp.int32)
    i32 = lambda v: v.astype(jnp.int32)
    loff_rows = jnp.broadcast_to(jnp.pad(i32(loff), ((0, 0), (0, LANE - N_EXPERTS)))[:, None, :],
                                 (n_tiles, 8, LANE))
    return dict(seg_dst=i32(seg_dst).reshape(-1), seg_cnt=i32(cnt8).reshape(-1), seg_loff=i32(loff).reshape(-1),
                fill_dst=i32(pad_start + tot8), fill_cnt=i32(padded - tot8), loff_rows=loff_rows,
                block_eid=block_eid, n_used=i32(pad_end[-1] // MOE_BLK).reshape(1), n_rows=n_blocks * MOE_BLK)


def _moe(x1, route_i, route_f, tile_cnt, w1, w3, w2, ln_g, ln_b, layer):
    n = x1.shape[0]
    counts = tile_cnt.reshape(n // POST_TM, 8, LANE)[:, 0, :N_EXPERTS]
    plan = _routing_plan(counts, n)
    xs = _dispatch(plan, x1, route_i, plan["n_rows"])
    ys = _expert_mlp(plan["block_eid"], plan["n_used"], xs, w1, w3, w2, layer)
    return _combine_norm(plan, x1, route_i, route_f, ys, ln_g, ln_b)


def _pad_heads(w):
    lead = w.shape[:-1]
    w = w.reshape(*lead, ML_HEADS, ML_DH)
    w = jnp.pad(w, [(0, 0)] * len(lead) + [(0, 0), (0, ML_DH_PAD - ML_DH)])
    return w.reshape(*lead, ML_W_PAD)


def _ml_in_weight(w):
    tw = TOKEN_W
    segs = [_pad_heads(w[:, i * tw:(i + 1) * tw]) for i in range(4)]
    gates = w[:, 4 * tw:4 * tw + 4 * ML_HEADS]
    qmem = w[:, 4 * tw + 4 * ML_HEADS:]
    gates = jnp.pad(gates, ((0, 0), (0, LANE - 4 * ML_HEADS)))
    return jnp.concatenate(segs + [qmem, gates], axis=1)


def _route_weight(wg, bg, we, be):
    w = jnp.concatenate([wg, we], axis=1)
    b = jnp.concatenate([bg, be], axis=0)
    pad = LANE - w.shape[1]
    return jnp.pad(w, ((0, 0), (0, pad))).astype(F32), jnp.pad(b, (0, pad)).reshape(1, LANE).astype(F32)


def kernel(x, mem, w_mem_kv, na_w_in, na_rpb, ml_w_in, ml_conv_w, ml_conv_b, ml_gate_b, ml_norm_g, w_out,
           ln1_g, ln1_b, ln2_g, ln2_b, moe_wg, moe_bg, moe_we, moe_be, moe_w1, moe_w3, moe_w2):
    batch, seq, d = x.shape
    n = batch * seq
    rows = seq // GRID_W
    xf = x.reshape(n, d)

    mem_kv = _matmul(mem.reshape(batch * N_MEM, d), w_mem_kv.astype(BF16), BF16,
                     min(PROJ_TM, batch * N_MEM), 2 * MEM_W, "mem_kv").reshape(batch, N_MEM, 2 * MEM_W)

    for layer in range(DEPTH):
        j = layer // 2
        w_route, b_route = _route_weight(moe_wg[layer], moe_bg[layer], moe_we[layer], moe_be[layer])
        g1 = ln1_g[layer].reshape(1, d)
        b1 = ln1_b[layer].reshape(1, d)
        if layer % 2 == 0:
            proj = _matmul(xf, na_w_in[j].astype(BF16), BF16, PROJ_TM, PROJ_TN, "na_in_proj")
            tok = _neighbourhood_attention(proj.reshape(batch, seq, -1), _na_bias_table(na_rpb[j], rows),
                                           batch, seq)
            x1, route_i, route_f, tile_cnt = _post_mixer((tok.reshape(n, TOKEN_W), proj), xf, mem_kv,
                                               w_out[layer].astype(BF16), g1, b1, w_route, b_route, seq, ml=False)
        else:
            proj = _matmul(xf, _ml_in_weight(ml_w_in[j]).astype(BF16), F32, PROJ_TM, PROJ_TN, "ml_in_proj")
            proj3 = proj.reshape(batch, seq, -1)
            conv_w = jnp.concatenate([_pad_heads(ml_conv_w[j][:, :TOKEN_W]),
                                      _pad_heads(ml_conv_w[j][:, TOKEN_W:])], axis=1)
            conv_b = jnp.concatenate([_pad_heads(ml_conv_b[j][:TOKEN_W]),
                                      _pad_heads(ml_conv_b[j][TOKEN_W:])]).reshape(1, 2 * ML_W_PAD)
            col_scale = jnp.concatenate([jnp.full((ML_W_PAD,), ML_DH ** -0.5, F32),
                                         jnp.ones((ML_W_PAD,), F32)]).reshape(1, 2 * ML_W_PAD)
            qk = _conv_silu(proj3, conv_w, conv_b, col_scale, batch, seq)
            gate_b = jnp.pad(ml_gate_b[j].reshape(-1), (0, LANE - 4 * ML_HEADS)).reshape(1, LANE)
            h_f, h_b = _mlstm_scan(qk, proj3, gate_b, batch, seq)
            norm_g = _pad_heads(ml_norm_g[j]).reshape(1, ML_W_PAD)
            wo = jnp.concatenate([_pad_heads(w_out[layer][:TOKEN_W].T).T, w_out[layer][TOKEN_W:]], axis=0)
            x1, route_i, route_f, tile_cnt = _post_mixer((h_f.reshape(n, ML_W_PAD), h_b.reshape(n, ML_W_PAD), proj, norm_g),
                                               xf, mem_kv, wo.astype(BF16), g1, b1, w_route, b_route, seq, ml=True)
        xf = _moe(x1, route_i, route_f, tile_cnt, moe_w1, moe_w3, moe_w2,
                  ln2_g[layer].reshape(1, d), ln2_b[layer].reshape(1, d), layer)
    return xf.reshape(batch, seq, d)
```

```python
import functools
import math

import jax
import jax.numpy as jnp
from jax import lax
from jax.experimental import pallas as pl
from jax.experimental.pallas import tpu as pltpu

D_MODEL = 1024
DEPTH = 4
N_MEM = 256
GRID_W = 64
TOKEN_W = 768
MEM_W = 256
HEAD_DIM = 64
NA_HEADS = 12
WIN_H = 8
WIN_W = 16
ML_HEADS = 4
ML_DH = 192
ML_CHUNK = 64
CONV_K = 5
MEM_HEADS = 4
MEM_DH = 64
N_GROUPS = 4
EXPERTS_PER_GROUP = 8
N_EXPERTS = 32
TOP_K = 2
D_EXPERT = 512
LN_EPS = 1e-5
ALPHA = (2 * DEPTH) ** 0.25

LANE = 128
ML_DH_PAD = 256
ML_W_PAD = ML_HEADS * ML_DH_PAD
VMEM_LIMIT_BYTES = 56 * 1024 * 1024

PROJ_TM = 512
PROJ_TN = 640
NA_ROW_UNROLL = 8
POST_TM = 512
MOE_BLK = 256
SEG_ALIGN = 8
SEG_BITS = (TOP_K * POST_TM // SEG_ALIGN).bit_length()
SEG_SMALL_BITS = 4
PACKED_W = D_MODEL // 2
LBUF = -(-(TOP_K * POST_TM + N_EXPERTS * (SEG_ALIGN - 1)) // 256) * 256
NEG_BIG = -1e30

F32 = jnp.float32
BF16 = jnp.bfloat16


def _cparams(*sem):
    return pltpu.CompilerParams(dimension_semantics=sem, vmem_limit_bytes=VMEM_LIMIT_BYTES)


def _matmul_kernel(x_ref, w_ref, o_ref, *, tn):
    x = x_ref[...].astype(BF16)
    for j in range(w_ref.shape[1] // tn):
        cols = slice(j * tn, (j + 1) * tn)
        o_ref[:, cols] = jnp.dot(x, w_ref[:, cols], preferred_element_type=F32).astype(o_ref.dtype)


def _matmul(x, w, out_dtype, tm, tn, name):
    m, k = x.shape
    n = w.shape[1]
    assert m % tm == 0 and n % tn == 0
    return pl.pallas_call(
        functools.partial(_matmul_kernel, tn=tn),
        grid=(m // tm,),
        in_specs=[pl.BlockSpec((tm, k), lambda i: (i, 0)),
                  pl.BlockSpec((k, n), lambda i: (0, 0))],
        out_specs=pl.BlockSpec((tm, n), lambda i: (i, 0)),
        out_shape=jax.ShapeDtypeStruct((m, n), out_dtype),
        compiler_params=_cparams("parallel"),
        name=name,
    )(x, w)


def _na_bias_table(rpb, rows):
    kh = min(WIN_H, rows)
    off = jnp.arange(kh)[:, None]
    j = jnp.arange(kh)[None, :]
    ri = j - off + (WIN_H - 1)
    c = jnp.arange(GRID_W)[:, None]
    kc = jnp.arange(GRID_W)[None, :]
    cs = jnp.clip(c - WIN_W // 2, 0, GRID_W - WIN_W)
    valid = (kc >= cs) & (kc < cs + WIN_W)
    ci = kc - c + (WIN_W - 1)
    row_sel = (ri[:, :, None] == jnp.arange(2 * WIN_H - 1)).astype(F32)
    col_sel = (ci[:, :, None] == jnp.arange(2 * WIN_W - 1)).astype(F32)
    tab = jnp.einsum('hab,oja,ckb->hocjk', rpb.astype(F32), row_sel, col_sel,
                     precision=lax.Precision.HIGHEST)
    tab = jnp.where(valid[None, None, :, None, :], tab, NEG_BIG)
    tab = tab.reshape(NA_HEADS // 2, 2, kh, GRID_W, kh * GRID_W)
    tab = tab.transpose(0, 2, 1, 3, 4).reshape(NA_HEADS // 2, kh, 2 * GRID_W, kh * GRID_W)
    return tab.astype(F32)


def _na_kernel(q_ref, k_ref, v_ref, b_ref, o_ref, *, rows, kh):
    lane = lax.broadcasted_iota(jnp.int32, (GRID_W, LANE), 1)
    first = lane < HEAD_DIM
    scale = HEAD_DIM ** -0.5

    def body(r, carry):
        rs = jnp.clip(r - kh // 2, 0, rows - kh)
        off = r - rs
        q = q_ref[0, pl.ds(pl.multiple_of(r * GRID_W, GRID_W), GRID_W), :]
        kb = k_ref[0, pl.ds(pl.multiple_of(rs * GRID_W, GRID_W), kh * GRID_W), :]
        vb = v_ref[0, pl.ds(pl.multiple_of(rs * GRID_W, GRID_W), kh * GRID_W), :]
        zero = jnp.zeros_like(q)
        qq = jnp.concatenate([jnp.where(first, q, zero), jnp.where(first, zero, q)], axis=0)
        s = lax.dot_general(qq, kb, (((1,), (1,)), ((), ())), preferred_element_type=F32)
        s = s * scale + b_ref[0, off]
        m = jnp.max(s, axis=-1, keepdims=True)
        p = jnp.exp(s - m)
        l = jnp.sum(p, axis=-1, keepdims=True)
        o = jnp.dot(p.astype(BF16), vb, preferred_element_type=F32) / l
        out = jnp.where(first, o[:GRID_W], o[GRID_W:])
        o_ref[0, pl.ds(pl.multiple_of(r * GRID_W, GRID_W), GRID_W), :] = out.astype(o_ref.dtype)
        return carry

    lax.fori_loop(0, rows, body, 0, unroll=NA_ROW_UNROLL)


def _neighbourhood_attention(proj, bias_tab, batch, seq):
    rows = seq // GRID_W
    kh = min(WIN_H, rows)
    npair = NA_HEADS // 2
    kern = functools.partial(_na_kernel, rows=rows, kh=kh)
    return pl.pallas_call(
        kern,
        grid=(batch, npair),
        in_specs=[pl.BlockSpec((1, seq, LANE), lambda b, h: (b, 0, h)),
                  pl.BlockSpec((1, seq, LANE), lambda b, h: (b, 0, npair + h)),
                  pl.BlockSpec((1, seq, LANE), lambda b, h: (b, 0, 2 * npair + h)),
                  pl.BlockSpec((1, kh, 2 * GRID_W, kh * GRID_W), lambda b, h: (h, 0, 0, 0))],
        out_specs=pl.BlockSpec((1, seq, LANE), lambda b, h: (b, 0, h)),
        out_shape=jax.ShapeDtypeStruct((batch, seq, TOKEN_W), BF16),
        compiler_params=_cparams("parallel", "arbitrary"),
        name="na_attention",
    )(proj, proj, proj, bias_tab)


def _conv_silu_kernel(x_ref, w_ref, b_ref, s_ref, o_ref, *, seq):
    x = x_ref[0]
    t = lax.broadcasted_iota(jnp.int32, x.shape, 0)
    acc = jnp.zeros_like(x) + b_ref[...]
    for j in range(CONV_K):
        sh = CONV_K // 2 - j
        if sh == 0:
            xs = x
        elif sh > 0:
            xs = jnp.where(t >= sh, pltpu.roll(x, sh, 0), 0.0)
        else:
            xs = jnp.where(t < seq + sh, pltpu.roll(x, seq + sh, 0), 0.0)
        acc = acc + w_ref[j:j + 1, :] * xs
    y = acc * jax.nn.sigmoid(acc) * s_ref[...]
    o_ref[0] = y.astype(o_ref.dtype)


def _conv_silu(proj, conv_w, conv_b, col_scale, batch, seq):
    ncol = 2 * ML_W_PAD // LANE
    kern = functools.partial(_conv_silu_kernel, seq=seq)
    return pl.pallas_call(
        kern,
        grid=(batch, ncol),
        in_specs=[pl.BlockSpec((1, seq, LANE), lambda b, j: (b, 0, j)),
                  pl.BlockSpec((CONV_K, LANE), lambda b, j: (0, j)),
                  pl.BlockSpec((1, LANE), lambda b, j: (0, j)),
                  pl.BlockSpec((1, LANE), lambda b, j: (0, j))],
        out_specs=pl.BlockSpec((1, seq, LANE), lambda b, j: (b, 0, j)),
        out_shape=jax.ShapeDtypeStruct((batch, seq, 2 * ML_W_PAD), BF16),
        compiler_params=_cparams("parallel", "arbitrary"),
        name="ml_conv_silu",
    )(proj, conv_w, conv_b, col_scale)


def _log_sigmoid(x):
    return jnp.minimum(x, 0.0) - jnp.log1p(jnp.exp(-jnp.abs(x)))


def _mlstm_chunk(q_ref, k_ref, v_ref, g_ref, gb_ref, o_ref, c_ref, m_ref, reverse):
    L = ML_CHUNK
    d = 1 if reverse else 0
    g = g_ref[0] + gb_ref[...]
    glane = lax.broadcasted_iota(jnp.int32, g.shape, 1)
    is_f = (glane >= 2 * ML_HEADS) & (glane < 4 * ML_HEADS)
    act = jnp.where(is_f, _log_sigmoid(g), g)
    row = lax.broadcasted_iota(jnp.int32, (L, L), 0)
    col = lax.broadcasted_iota(jnp.int32, (L, L), 1)
    valid = (col >= row) if reverse else (col <= row)
    bcum = jnp.dot(valid.astype(F32), act, preferred_element_type=F32,
                   precision=lax.Precision.HIGHEST)
    act_t = act.T
    bcum_t = bcum.T
    last = 0 if reverse else L - 1
    hlane = lax.broadcasted_iota(jnp.int32, (L, ML_DH_PAD), 1)

    for h in range(ML_HEADS):
        li_lane = d * ML_HEADS + h
        lf_lane = 2 * ML_HEADS + d * ML_HEADS + h
        b_col = bcum[:, lf_lane:lf_lane + 1]
        b_row = bcum_t[lf_lane:lf_lane + 1, :]
        li_col = act[:, li_lane:li_lane + 1]
        li_row = act_t[li_lane:li_lane + 1, :]
        dm = jnp.where(valid, b_col - b_row + li_row, NEG_BIG)
        m_prev = m_ref[h][0:1, 0:1]
        inter = b_col + m_prev
        m_t = jnp.maximum(inter, jnp.max(dm, axis=-1, keepdims=True))
        w_inter = jnp.exp(inter - m_t)

        sl = slice(h * ML_DH_PAD, (h + 1) * ML_DH_PAD)
        qh = q_ref[0, :, sl]
        kh = k_ref[0, :, sl]
        vh = jnp.where(hlane == ML_DH, 1.0, v_ref[0, :, sl]).astype(BF16)
        s = lax.dot_general(qh, kh, (((1,), (1,)), ((), ())), preferred_element_type=F32)
        s = s * jnp.exp(dm - m_t)
        c_old = c_ref[h]
        num = (w_inter * jnp.dot(qh, c_old.astype(BF16), preferred_element_type=F32)
               + jnp.dot(s.astype(BF16), vh, preferred_element_type=F32))
        den = num[:, ML_DH:ML_DH + 1]
        hout = num / jnp.maximum(jnp.abs(den), jnp.exp(-m_t))
        o_ref[0, :, sl] = jnp.where(hlane < ML_DH, hout, 0.0)

        b_last = b_col[last:last + 1, :]
        m_new = m_t[last:last + 1, :]
        decay = jnp.exp(b_last + m_prev - m_new)
        w_src = jnp.exp(b_last - b_col + li_col - m_new)
        kw = (kh.astype(F32) * w_src).astype(BF16)
        upd = lax.dot_general(kw, vh, (((0,), (0,)), ((), ())), preferred_element_type=F32)
        c_ref[h] = decay * c_old + upd
        m_ref[h] = jnp.zeros(m_ref.shape[1:], F32) + m_new


def _mlstm_kernel(qf_ref, kf_ref, vf_ref, gf_ref, qb_ref, kb_ref, vb_ref, gr_ref, gb_ref,
                  of_ref, ob_ref, c_ref, m_ref):
    @pl.when(pl.program_id(1) == 0)
    def _():
        c_ref[...] = jnp.zeros_like(c_ref)
        m_ref[...] = jnp.zeros_like(m_ref)

    _mlstm_chunk(qf_ref, kf_ref, vf_ref, gf_ref, gb_ref, of_ref, c_ref.at[0], m_ref.at[0], reverse=False)
    _mlstm_chunk(qb_ref, kb_ref, vb_ref, gr_ref, gb_ref, ob_ref, c_ref.at[1], m_ref.at[1], reverse=True)


def _mlstm_scan(qk, proj, gate_b, batch, seq):
    nc = seq // ML_CHUNK
    gcol = (4 * ML_W_PAD + MEM_W) // LANE
    fwd = lambda col: (lambda b, c: (b, c, col))
    bwd = lambda col: (lambda b, c: (b, nc - 1 - c, col))
    wide = lambda m: pl.BlockSpec((1, ML_CHUNK, ML_W_PAD), m)
    gate = lambda m: pl.BlockSpec((1, ML_CHUNK, LANE), m)
    out = jax.ShapeDtypeStruct((batch, seq, ML_W_PAD), F32)
    return pl.pallas_call(
        _mlstm_kernel,
        grid=(batch, nc),
        in_specs=[wide(fwd(0)), wide(fwd(1)), wide(fwd(2)), gate(fwd(gcol)),
                  wide(bwd(0)), wide(bwd(1)), wide(bwd(2)), gate(bwd(gcol)),
                  pl.BlockSpec((1, LANE), lambda b, c: (0, 0))],
        out_specs=[wide(fwd(0)), wide(bwd(0))],
        out_shape=[out, out],
        scratch_shapes=[pltpu.VMEM((2, ML_HEADS, ML_DH_PAD, ML_DH_PAD), F32),
                        pltpu.VMEM((2, ML_HEADS, 8, LANE), F32)],
        compiler_params=_cparams("parallel", "arbitrary"),
        name="mlstm_scan",
    )(qk, qk, proj, proj, qk, qk, proj, proj, gate_b)


def _layer_norm(y, g, b):
    mu = jnp.mean(y, axis=-1, keepdims=True)
    yc = y - mu
    var = jnp.mean(yc * yc, axis=-1, keepdims=True)
    return yc * lax.rsqrt(var + LN_EPS) * g + b


def _first_argmax(val, lane):
    mx = jnp.max(val, axis=-1, keepdims=True)
    idx = jnp.min(jnp.where(val == mx, lane.astype(F32), float(LANE)), axis=-1, keepdims=True)
    return mx, idx.astype(jnp.int32)


def _route(x1, wr_ref, br_ref):
    x_hi = x1.astype(BF16)
    x_lo = (x1 - x_hi.astype(F32)).astype(BF16)
    t = jnp.dot(x_hi, wr_ref[...], preferred_element_type=F32)
    logits = (t[:, :LANE] + t[:, LANE:] + jnp.dot(x_lo, wr_ref[:, :LANE], preferred_element_type=F32)
              + br_ref[...])
    lane = lax.broadcasted_iota(jnp.int32, logits.shape, 1)
    is_g = lane < N_GROUPS
    gl = jnp.where(is_g, logits, NEG_BIG)
    gmax, gidx = _first_argmax(gl, lane)
    gsum = jnp.sum(jnp.where(is_g, jnp.exp(gl - gmax), 0.0), axis=-1, keepdims=True)
    g_gate = 1.0 / gsum
    lo = N_GROUPS + gidx * EXPERTS_PER_GROUP
    in_grp = (lane >= lo) & (lane < lo + EXPERTS_PER_GROUP)
    el = jnp.where(in_grp, logits, NEG_BIG)
    emax = jnp.max(el, axis=-1, keepdims=True)
    pe = jnp.where(in_grp, jnp.exp(el - emax), 0.0)
    prob = pe / jnp.sum(pe, axis=-1, keepdims=True)
    prob = jnp.where(in_grp, prob, -1.0)
    p1, i1 = _first_argmax(prob, lane)
    p2, i2 = _first_argmax(jnp.where(lane == i1, -1.0, prob), lane)
    tot = p1 + p2
    return i1 - N_GROUPS, i2 - N_GROUPS, g_gate * p1 / tot, g_gate * p2 / tot


def _post_kernel(*refs, ml):
    if ml:
        (hf_ref, hb_ref, op_ref, ng_ref, qm_ref, x_ref, mkv_ref, wo_ref, g_ref, b_ref, wr_ref, br_ref,
         x1_ref, ri_ref, rf_ref, cnt_ref) = refs
        hsum = hf_ref[...] + hb_ref[...]
        hl = lax.broadcasted_iota(jnp.int32, (hsum.shape[0], ML_DH_PAD), 1)
        parts = []
        for h in range(ML_HEADS):
            hh = hsum[:, h * ML_DH_PAD:(h + 1) * ML_DH_PAD]
            mu = jnp.sum(hh, axis=-1, keepdims=True) * (1.0 / ML_DH)
            hc = jnp.where(hl < ML_DH, hh - mu, 0.0)
            var = jnp.sum(hc * hc, axis=-1, keepdims=True) * (1.0 / ML_DH)
            parts.append(hc * lax.rsqrt(var + LN_EPS))
        hn = jnp.concatenate(parts, axis=-1) * ng_ref[...]
        tok = (jax.nn.sigmoid(op_ref[...]) * hn).astype(BF16)
    else:
        (tok_ref, qm_ref, x_ref, mkv_ref, wo_ref, g_ref, b_ref, wr_ref, br_ref,
         x1_ref, ri_ref, rf_ref, cnt_ref) = refs
        tok = tok_ref[...]

    qm = qm_ref[...].astype(BF16)
    mk = mkv_ref[0, :, 0:MEM_W]
    mv = mkv_ref[0, :, MEM_W:2 * MEM_W]
    mlane = lax.broadcasted_iota(jnp.int32, qm.shape, 1)
    matt = jnp.zeros(qm.shape, F32)
    for h in range(MEM_HEADS):
        hm = (mlane >= h * MEM_DH) & (mlane < (h + 1) * MEM_DH)
        qh = jnp.where(hm, qm, jnp.zeros_like(qm))
        s = lax.dot_general(qh, mk, (((1,), (1,)), ((), ())), preferred_element_type=F32) * (MEM_DH ** -0.5)
        p = jnp.exp(s - jnp.max(s, axis=-1, keepdims=True))
        l = jnp.sum(p, axis=-1, keepdims=True)
        o = jnp.dot(p.astype(BF16), mv, preferred_element_type=F32) / l
        matt = jnp.where(hm, o, matt)

    mixed = jnp.concatenate([tok, matt.astype(BF16)], axis=-1)
    y = ALPHA * x_ref[...] + jnp.dot(mixed, wo_ref[...], preferred_element_type=F32)
    x1 = _layer_norm(y, g_ref[...], b_ref[...])
    x1_ref[...] = x1

    e1, e2, g1, g2 = _route(x1, wr_ref, br_ref)
    lane = lax.broadcasted_iota(jnp.int32, ri_ref.shape, 1)
    tm = x1.shape[0]
    oh1 = jnp.where(lane == e1, 1.0, 0.0)
    oh2 = jnp.where(lane == e2, 1.0, 0.0)
    trow = lax.broadcasted_iota(jnp.int32, (tm, tm), 0)
    tcol = lax.broadcasted_iota(jnp.int32, (tm, tm), 1)
    earlier = jnp.where(tcol < trow, 1.0, 0.0).astype(BF16)
    before1 = jnp.dot(earlier, oh1.astype(BF16), preferred_element_type=F32)
    before2 = jnp.dot(earlier, oh2.astype(BF16), preferred_element_type=F32)
    c1 = jnp.sum(oh1, axis=0, keepdims=True)
    c2 = jnp.sum(oh2, axis=0, keepdims=True)
    rank1 = jnp.sum(oh1 * before1, axis=-1, keepdims=True).astype(jnp.int32)
    rank2 = jnp.sum(oh2 * (before2 + c1), axis=-1, keepdims=True).astype(jnp.int32)
    ri_ref[...] = jnp.where(lane == 0, e1, jnp.where(lane == 1, e2,
                            jnp.where(lane == 2, rank1, jnp.where(lane == 3, rank2, 0))))
    rf_ref[...] = jnp.where(lane == 0, g1, jnp.where(lane == 1, g2, 0.0))
    cnt_ref[...] = jnp.broadcast_to(c1 + c2, cnt_ref.shape).astype(jnp.int32)


def _post_mixer(mix_inputs, x, mem_kv, w_out, ln_g, ln_b, w_route, b_route, seq, ml):
    n = x.shape[0]
    tm = POST_TM
    tiles_per_batch = seq // tm
    row = lambda i: (i, 0)
    const = lambda i: (0, 0)
    if ml:
        h_f, h_b, proj, norm_g = mix_inputs
        ins = [h_f, h_b, proj, norm_g, proj]
        specs = [pl.BlockSpec((tm, ML_W_PAD), row),
                 pl.BlockSpec((tm, ML_W_PAD), row),
                 pl.BlockSpec((tm, ML_W_PAD), lambda i: (i, 3)),
                 pl.BlockSpec((1, ML_W_PAD), const),
                 pl.BlockSpec((tm, MEM_W), lambda i: (i, 4 * ML_W_PAD // MEM_W))]
    else:
        tok, proj = mix_inputs
        ins = [tok, proj]
        specs = [pl.BlockSpec((tm, TOKEN_W), row),
                 pl.BlockSpec((tm, MEM_W), lambda i: (i, 3 * TOKEN_W // MEM_W))]
    ins += [x, mem_kv, w_out, ln_g, ln_b, w_route, b_route]
    specs += [pl.BlockSpec((tm, D_MODEL), row),
              pl.BlockSpec((1, N_MEM, 2 * MEM_W), lambda i: (i // tiles_per_batch, 0, 0)),
              pl.BlockSpec(w_out.shape, const),
              pl.BlockSpec((1, D_MODEL), const),
              pl.BlockSpec((1, D_MODEL), const),
              pl.BlockSpec((D_MODEL, 2 * LANE), const),
              pl.BlockSpec((1, LANE), const)]
    return pl.pallas_call(
        functools.partial(_post_kernel, ml=ml),
        grid=(n // tm,),
        in_specs=specs,
        out_specs=[pl.BlockSpec((tm, D_MODEL), row),
                   pl.BlockSpec((tm, LANE), row),
                   pl.BlockSpec((tm, LANE), row),
                   pl.BlockSpec((8, LANE), row)],
        out_shape=[jax.ShapeDtypeStruct((n, D_MODEL), F32),
                   jax.ShapeDtypeStruct((n, LANE), jnp.int32),
                   jax.ShapeDtypeStruct((n, LANE), F32),
                   jax.ShapeDtypeStruct((n // tm * 8, LANE), jnp.int32)],
        compiler_params=_cparams("parallel"),
        name="post_mixer_ml" if ml else "post_mixer_na",
    )(*ins)


def _start_segment_copies(tile, dst_ref, cnt_ref, loff_ref, make_copy):
    def seg(e, carry):
        idx = tile * N_EXPERTS + e
        n8 = cnt_ref[idx]
        l0 = loff_ref[idx]
        g0 = dst_ref[idx]

        def piece(b):
            size = SEG_ALIGN << b
            done = n8 & (size - 1)

            @pl.when((n8 & size) != 0)
            def _():
                make_copy(pl.multiple_of(l0 + done, SEG_ALIGN), pl.multiple_of(g0 + done, SEG_ALIGN), size).start()

        for b in range(SEG_SMALL_BITS):
            piece(b)

        @pl.when(n8 >= (SEG_ALIGN << SEG_SMALL_BITS))
        def _():
            for b in range(SEG_SMALL_BITS, SEG_BITS):
                piece(b)
        return carry

    lax.fori_loop(0, N_EXPERTS, seg, 0)


def _wait_rows(total, make_desc):
    for b in range((LBUF // SEG_ALIGN).bit_length()):
        size = SEG_ALIGN << b

        @pl.when((total & size) != 0)
        def _():
            make_desc(size).wait()


def _pack_bf16_pairs(x, rounded):
    if not rounded:
        x = x.astype(BF16).astype(F32)
    lo = lax.bitcast_convert_type(x[:, :PACKED_W], jnp.uint32) >> 16
    hi = lax.bitcast_convert_type(x[:, PACKED_W:], jnp.uint32) & jnp.uint32(0xFFFF0000)
    return lo | hi


def _unpack_bf16_pairs(u):
    lo = lax.bitcast_convert_type(u << 16, F32)
    hi = lax.bitcast_convert_type(u & jnp.uint32(0xFFFF0000), F32)
    return jnp.concatenate([lo, hi], axis=-1).astype(BF16)


def _local_positions(ri, lrow):
    lane = lax.broadcasted_iota(jnp.int32, ri.shape, 1)
    pos = []
    for k in range(TOP_K):
        off = jnp.sum(jnp.where(lane == ri[:, k:k + 1], lrow, 0).astype(F32), axis=-1, keepdims=True)
        pos.append(off.astype(jnp.int32) + ri[:, TOP_K + k:TOP_K + k + 1])
    return pos


def _dispatch_kernel(dst_ref, cnt_ref, loff_ref, tot_ref, zdst_ref, zcnt_ref, nused_ref,
                     x_ref, ri_ref, lrow_ref, xs_ref, sbuf, zbuf, sems):
    t = pl.program_id(0)
    cur = lax.rem(t, 2)
    tm = x_ref.shape[0]
    p0, p1 = _local_positions(ri_ref[...], lrow_ref[0, 0:1, :])
    lane = lax.broadcasted_iota(jnp.int32, (tm, LANE), 1)
    both = jnp.where(lane == 0, p0, jnp.where(lane == 1, p1, -1)).astype(F32)
    both_t = both.T.astype(jnp.int32)
    row = lax.broadcasted_iota(jnp.int32, (LBUF, tm), 0)
    perm = jnp.where((row == both_t[0:1, :]) | (row == both_t[1:2, :]), 1.0, 0.0).astype(BF16)
    sbuf[cur] = _pack_bf16_pairs(jnp.dot(perm, x_ref[...].astype(BF16), preferred_element_type=F32), rounded=True)

    def seg_copy(slot):
        def make(lrow, grow, n):
            return pltpu.make_async_copy(sbuf.at[slot, pl.ds(lrow, n)], xs_ref.at[pl.ds(grow, n)], sems.at[slot])
        return make

    _start_segment_copies(t, dst_ref, cnt_ref, loff_ref, seg_copy(cur))

    @pl.when(t > 0)
    def _():
        _wait_rows(tot_ref[t - 1], lambda n: seg_copy(1 - cur)(0, 0, n))

    @pl.when(t == pl.num_programs(0) - 1)
    def _():
        _wait_rows(tot_ref[t], lambda n: seg_copy(cur)(0, 0, n))
        sem = sems.at[0]
        zbuf[...] = jnp.zeros_like(zbuf)
        for wait in (False, True):
            def fill(e, carry):
                n8 = zcnt_ref[e]
                g0 = zdst_ref[e]
                for b in range((MOE_BLK // SEG_ALIGN).bit_length() - 1):
                    size = SEG_ALIGN << b
                    done = n8 & (size - 1)

                    @pl.when((n8 & size) != 0)
                    def _():
                        cp = pltpu.make_async_copy(zbuf.at[pl.ds(0, size)],
                                                   xs_ref.at[pl.ds(pl.multiple_of(g0 + done, SEG_ALIGN), size)], sem)
                        if wait:
                            cp.wait()
                        else:
                            cp.start()
                return carry
            lax.fori_loop(0, N_EXPERTS, fill, 0)

            def fill_tail(h, carry):
                cp = pltpu.make_async_copy(zbuf, xs_ref.at[pl.ds(pl.multiple_of(h * zbuf.shape[0], SEG_ALIGN),
                                                                 zbuf.shape[0])], sem)
                if wait:
                    cp.wait()
                else:
                    cp.start()
                return carry
            halves = MOE_BLK // zbuf.shape[0]
            lax.fori_loop(nused_ref[0] * halves, xs_ref.shape[0] // zbuf.shape[0], fill_tail, 0)


def _dispatch(plan, x1, route_i, n_rows):
    n = x1.shape[0]
    tm = POST_TM
    row = lambda i, *_: (i, 0)
    return pl.pallas_call(
        _dispatch_kernel,
        grid_spec=pltpu.PrefetchScalarGridSpec(
            num_scalar_prefetch=7,
            grid=(n // tm,),
            in_specs=[pl.BlockSpec((tm, D_MODEL), row),
                      pl.BlockSpec((tm, LANE), row),
                      pl.BlockSpec((1, 8, LANE), lambda i, *_: (i, 0, 0)),
                      ],
            out_specs=pl.BlockSpec(memory_space=pl.ANY),
            scratch_shapes=[pltpu.VMEM((2, LBUF, PACKED_W), jnp.uint32),
                            pltpu.VMEM((MOE_BLK // 2, PACKED_W), jnp.uint32),
                            pltpu.SemaphoreType.DMA((2,))]),
        out_shape=jax.ShapeDtypeStruct((n_rows, PACKED_W), jnp.uint32),
        compiler_params=_cparams("arbitrary"),
        name="moe_dispatch",
    )(plan["seg_dst"], plan["seg_cnt"], plan["seg_loff"], plan["seg_tot"], plan["fill_dst"], plan["fill_cnt"],
      plan["n_used"], x1, route_i, plan["loff_rows"])


def _expert_kernel(beid_ref, nused_ref, x_ref, w1_ref, w3_ref, w2_ref, o_ref, w1b, w3b, w2b):
    i = pl.program_id(0)
    prev = beid_ref[jnp.maximum(i - 1, 0)]
    fresh = (i == 0) | (beid_ref[i] != prev)

    @pl.when(fresh)
    def _():
        w1b[...] = w1_ref[...].astype(BF16)
        w3b[...] = w3_ref[...].astype(BF16)
        w2b[...] = w2_ref[...].astype(BF16)

    @pl.when(i < nused_ref[0])
    def _():
        x = _unpack_bf16_pairs(x_ref[...])
        a = jnp.dot(x, w1b[...], preferred_element_type=F32)
        b = jnp.dot(x, w3b[...], preferred_element_type=F32)
        hmid = (a * jax.nn.sigmoid(a) * b).astype(BF16)
        o_ref[...] = _pack_bf16_pairs(jnp.dot(hmid, w2b[...], preferred_element_type=F32), rounded=False)

    @pl.when(i >= nused_ref[0])
    def _():
        o_ref[...] = jnp.zeros_like(o_ref)


def _expert_mlp(block_eid, n_used, xs, w1, w3, w2, layer):
    p = xs.shape[0]
    nb = p // MOE_BLK
    wmap13 = lambda i, be, nu: (layer, be[i], 0, 0)
    return pl.pallas_call(
        _expert_kernel,
        grid_spec=pltpu.PrefetchScalarGridSpec(
            num_scalar_prefetch=2,
            grid=(nb,),
            in_specs=[pl.BlockSpec((MOE_BLK, PACKED_W), lambda i, be, nu: (jnp.maximum(jnp.minimum(i, nu[0] - 1), 0), 0)),
                      pl.BlockSpec((None, None, D_MODEL, D_EXPERT), wmap13),
                      pl.BlockSpec((None, None, D_MODEL, D_EXPERT), wmap13),
                      pl.BlockSpec((None, None, D_EXPERT, D_MODEL), wmap13)],
            out_specs=pl.BlockSpec((MOE_BLK, PACKED_W), lambda i, be, nu: (i, 0)),
            scratch_shapes=[pltpu.VMEM((D_MODEL, D_EXPERT), BF16),
                            pltpu.VMEM((D_MODEL, D_EXPERT), BF16),
                            pltpu.VMEM((D_EXPERT, D_MODEL), BF16)]),
        out_shape=jax.ShapeDtypeStruct((p, PACKED_W), jnp.uint32),
        compiler_params=_cparams("arbitrary"),
        name="moe_experts",
    )(block_eid, n_used, xs, w1, w3, w2)


def _combine_kernel(dst_ref, cnt_ref, loff_ref, tot_ref, x_ref, ri_ref, rf_ref, lrow_ref, g_ref, b_ref, ys_ref,
                    o_ref, ybuf, sems):
    t = pl.program_id(0)
    cur = lax.rem(t, 2)

    def seg_copy(slot):
        def make(lrow, grow, n):
            return pltpu.make_async_copy(ys_ref.at[pl.ds(grow, n)], ybuf.at[slot, pl.ds(lrow, n)], sems.at[slot])
        return make

    @pl.when(t == 0)
    def _():
        ybuf[...] = jnp.zeros_like(ybuf)
        _start_segment_copies(t, dst_ref, cnt_ref, loff_ref, seg_copy(cur))

    @pl.when(t + 1 < pl.num_programs(0))
    def _():
        _start_segment_copies(t + 1, dst_ref, cnt_ref, loff_ref, seg_copy(1 - cur))

    p0, p1 = _local_positions(ri_ref[...], lrow_ref[0, 0:1, :])
    col = lax.broadcasted_iota(jnp.int32, (x_ref.shape[0], LBUF), 1)
    sel0 = jnp.where(col == p0, 1.0, 0.0).astype(BF16)
    sel1 = jnp.where(col == p1, 1.0, 0.0).astype(BF16)
    _wait_rows(tot_ref[t], lambda n: seg_copy(cur)(0, 0, n))

    yb = _unpack_bf16_pairs(ybuf[cur])
    y0 = jnp.dot(sel0, yb, preferred_element_type=F32)
    y1 = jnp.dot(sel1, yb, preferred_element_type=F32)
    ffn = y0 * rf_ref[:, 0:1] + y1 * rf_ref[:, 1:2]
    o_ref[...] = _layer_norm(ALPHA * x_ref[...] + ffn, g_ref[...], b_ref[...])


def _combine_norm(plan, x1, route_i, route_f, ys, ln_g, ln_b):
    n = x1.shape[0]
    tm = POST_TM
    row = lambda i, *_: (i, 0)
    const = lambda i, *_: (0, 0)
    return pl.pallas_call(
        _combine_kernel,
        grid_spec=pltpu.PrefetchScalarGridSpec(
            num_scalar_prefetch=4,
            grid=(n // tm,),
            in_specs=[pl.BlockSpec((tm, D_MODEL), row),
                      pl.BlockSpec((tm, LANE), row),
                      pl.BlockSpec((tm, LANE), row),
                      pl.BlockSpec((1, 8, LANE), lambda i, *_: (i, 0, 0)),
                      pl.BlockSpec((1, D_MODEL), const),
                      pl.BlockSpec((1, D_MODEL), const),
                      pl.BlockSpec(memory_space=pl.ANY)],
            out_specs=pl.BlockSpec((tm, D_MODEL), row),
            scratch_shapes=[pltpu.VMEM((2, LBUF, PACKED_W), jnp.uint32),
                            pltpu.SemaphoreType.DMA((2,))]),
        out_shape=jax.ShapeDtypeStruct((n, D_MODEL), F32),
        compiler_params=_cparams("arbitrary"),
        name="moe_combine_norm",
    )(plan["seg_dst"], plan["seg_cnt"], plan["seg_loff"], plan["seg_tot"], x1, route_i, route_f,
      plan["loff_rows"], ln_g, ln_b, ys)


def _routing_plan(tile_counts, n_tokens):
    n_tiles = tile_counts.shape[0]
    e_before = jnp.arange(N_EXPERTS)[:, None] < jnp.arange(N_EXPERTS)[None, :]
    t_before = jnp.arange(n_tiles)[:, None] < jnp.arange(n_tiles)[None, :]
    cnt8 = (tile_counts + SEG_ALIGN - 1) // SEG_ALIGN * SEG_ALIGN
    loff = jnp.sum(jnp.where(e_before[None], cnt8[:, :, None], 0), axis=1)
    tot8 = jnp.sum(cnt8, axis=0)
    padded = (tot8 + MOE_BLK - 1) // MOE_BLK * MOE_BLK
    pad_start = jnp.sum(jnp.where(e_before, padded[:, None], 0), axis=0)
    pad_end = pad_start + padded
    seg_dst = pad_start[None, :] + jnp.sum(jnp.where(t_before[:, :, None], cnt8[:, None, :], 0), axis=0)
    worst_rows = n_tokens * TOP_K + n_tiles * N_EXPERTS * (SEG_ALIGN - 1) + N_EXPERTS * (MOE_BLK - 1)
    n_blocks = -(-worst_rows // MOE_BLK)
    block_row = jnp.arange(n_blocks, dtype=jnp.int32) * MOE_BLK
    block_eid = jnp.minimum(jnp.sum(pad_end[None, :] <= block_row[:, None], axis=1), N_EXPERTS - 1).astype(jnp.int32)
    i32 = lambda v: v.astype(jnp.int32)
    loff_rows = jnp.broadcast_to(jnp.pad(i32(loff), ((0, 0), (0, LANE - N_EXPERTS)))[:, None, :],
                                 (n_tiles, 8, LANE))
    return dict(seg_dst=i32(seg_dst).reshape(-1), seg_cnt=i32(cnt8).reshape(-1), seg_loff=i32(loff).reshape(-1),
                seg_tot=i32(jnp.sum(cnt8, axis=1)),
                fill_dst=i32(pad_start + tot8), fill_cnt=i32(padded - tot8), loff_rows=loff_rows,
                block_eid=block_eid, n_used=i32(pad_end[-1] // MOE_BLK).reshape(1), n_rows=n_blocks * MOE_BLK)


def _moe(x1, route_i, route_f, tile_cnt, w1, w3, w2, ln_g, ln_b, layer):
    n = x1.shape[0]
    counts = tile_cnt.reshape(n // POST_TM, 8, LANE)[:, 0, :N_EXPERTS]
    plan = _routing_plan(counts, n)
    xs = _dispatch(plan, x1, route_i, plan["n_rows"])
    ys = _expert_mlp(plan["block_eid"], plan["n_used"], xs, w1, w3, w2, layer)
    return _combine_norm(plan, x1, route_i, route_f, ys, ln_g, ln_b)


def _pad_heads(w):
    lead = w.shape[:-1]
    w = w.reshape(*lead, ML_HEADS, ML_DH)
    w = jnp.pad(w, [(0, 0)] * len(lead) + [(0, 0), (0, ML_DH_PAD - ML_DH)])
    return w.reshape(*lead, ML_W_PAD)


def _ml_in_weight(w):
    tw = TOKEN_W
    segs = [_pad_heads(w[:, i * tw:(i + 1) * tw]) for i in range(4)]
    gates = w[:, 4 * tw:4 * tw + 4 * ML_HEADS]
    qmem = w[:, 4 * tw + 4 * ML_HEADS:]
    gates = jnp.pad(gates, ((0, 0), (0, LANE - 4 * ML_HEADS)))
    return jnp.concatenate(segs + [qmem, gates], axis=1)


def _route_weight(wg, bg, we, be):
    w = jnp.concatenate([wg, we], axis=1)
    b = jnp.concatenate([bg, be], axis=0)
    pad = LANE - w.shape[1]
    w = jnp.pad(w, ((0, 0), (0, pad))).astype(F32)
    w_hi = w.astype(BF16)
    w_lo = (w - w_hi.astype(F32)).astype(BF16)
    return jnp.concatenate([w_hi, w_lo], axis=1), jnp.pad(b, (0, pad)).reshape(1, LANE).astype(F32)


def kernel(x, mem, w_mem_kv, na_w_in, na_rpb, ml_w_in, ml_conv_w, ml_conv_b, ml_gate_b, ml_norm_g, w_out,
           ln1_g, ln1_b, ln2_g, ln2_b, moe_wg, moe_bg, moe_we, moe_be, moe_w1, moe_w3, moe_w2):
    batch, seq, d = x.shape
    n = batch * seq
    rows = seq // GRID_W
    xf = x.reshape(n, d)

    mem_kv = _matmul(mem.reshape(batch * N_MEM, d), w_mem_kv.astype(BF16), BF16,
                     min(PROJ_TM, batch * N_MEM), 2 * MEM_W, "mem_kv").reshape(batch, N_MEM, 2 * MEM_W)

    for layer in range(DEPTH):
        j = layer // 2
        w_route, b_route = _route_weight(moe_wg[layer], moe_bg[layer], moe_we[layer], moe_be[layer])
        g1 = ln1_g[layer].reshape(1, d)
        b1 = ln1_b[layer].reshape(1, d)
        if layer % 2 == 0:
            proj = _matmul(xf, na_w_in[j].astype(BF16), BF16, PROJ_TM, PROJ_TN, "na_in_proj")
            tok = _neighbourhood_attention(proj.reshape(batch, seq, -1), _na_bias_table(na_rpb[j], rows),
                                           batch, seq)
            x1, route_i, route_f, tile_cnt = _post_mixer((tok.reshape(n, TOKEN_W), proj), xf, mem_kv,
                                               w_out[layer].astype(BF16), g1, b1, w_route, b_route, seq, ml=False)
        else:
            proj = _matmul(xf, _ml_in_weight(ml_w_in[j]).astype(BF16), F32, PROJ_TM, PROJ_TN, "ml_in_proj")
            proj3 = proj.reshape(batch, seq, -1)
            conv_w = jnp.concatenate([_pad_heads(ml_conv_w[j][:, :TOKEN_W]),
                                      _pad_heads(ml_conv_w[j][:, TOKEN_W:])], axis=1)
            conv_b = jnp.concatenate([_pad_heads(ml_conv_b[j][:TOKEN_W]),
                                      _pad_heads(ml_conv_b[j][TOKEN_W:])]).reshape(1, 2 * ML_W_PAD)
            col_scale = jnp.concatenate([jnp.full((ML_W_PAD,), ML_DH ** -0.5, F32),
                                         jnp.ones((ML_W_PAD,), F32)]).reshape(1, 2 * ML_W_PAD)
            qk = _conv_silu(proj3, conv_w, conv_b, col_scale, batch, seq)
            gate_b = jnp.pad(ml_gate_b[j].reshape(-1), (0, LANE - 4 * ML_HEADS)).reshape(1, LANE)
            h_f, h_b = _mlstm_scan(qk, proj3, gate_b, batch, seq)
            norm_g = _pad_heads(ml_norm_g[j]).reshape(1, ML_W_PAD)
            wo = jnp.concatenate([_pad_heads(w_out[layer][:TOKEN_W].T).T, w_out[layer][TOKEN_W:]], axis=0)
            x1, route_i, route_f, tile_cnt = _post_mixer((h_f.reshape(n, ML_W_PAD), h_b.reshape(n, ML_W_PAD), proj, norm_g),
                                               xf, mem_kv, wo.astype(BF16), g1, b1, w_route, b_route, seq, ml=True)
        xf = _moe(x1, route_i, route_f, tile_cnt, moe_w1, moe_w3, moe_w2,
                  ln2_g[layer].reshape(1, d), ln2_b[layer].reshape(1, d), layer)
    return xf.reshape(batch, seq, d)
```

```python
import functools
import math

import jax
import jax.numpy as jnp
from jax import lax
from jax.experimental import pallas as pl
from jax.experimental.pallas import tpu as pltpu

D_MODEL = 1024
DEPTH = 4
N_MEM = 256
GRID_W = 64
TOKEN_W = 768
MEM_W = 256
HEAD_DIM = 64
NA_HEADS = 12
WIN_H = 8
WIN_W = 16
ML_HEADS = 4
ML_DH = 192
ML_CHUNK = 256
CONV_K = 5
MEM_HEADS = 4
MEM_DH = 64
N_GROUPS = 4
EXPERTS_PER_GROUP = 8
N_EXPERTS = 32
TOP_K = 2
D_EXPERT = 512
LN_EPS = 1e-5
ALPHA = (2 * DEPTH) ** 0.25

LANE = 128
ML_DH_PAD = 256
ML_W_PAD = ML_HEADS * ML_DH_PAD
VMEM_LIMIT_BYTES = 56 * 1024 * 1024

PROJ_TM = 512
PROJ_TN = 640
NA_ROW_UNROLL = 8
POST_TM = 512
MOE_BLK = 512
ML_INTRA_CHUNKS = 1
SEG_ALIGN = 8
SEG_BITS = (TOP_K * POST_TM // SEG_ALIGN).bit_length()
SEG_SMALL_BITS = 4
PACKED_W = D_MODEL // 2
LBUF = -(-(TOP_K * POST_TM + N_EXPERTS * (SEG_ALIGN - 1)) // 256) * 256
NEG_BIG = -1e30

F32 = jnp.float32
BF16 = jnp.bfloat16


def _cparams(*sem):
    return pltpu.CompilerParams(dimension_semantics=sem, vmem_limit_bytes=VMEM_LIMIT_BYTES)


def _matmul_kernel(x_ref, w_ref, o_ref, *, tn):
    x = x_ref[...].astype(BF16)
    for j in range(w_ref.shape[1] // tn):
        cols = slice(j * tn, (j + 1) * tn)
        o_ref[:, cols] = jnp.dot(x, w_ref[:, cols], preferred_element_type=F32).astype(o_ref.dtype)


def _matmul(x, w, out_dtype, tm, tn, name):
    m, k = x.shape
    n = w.shape[1]
    assert m % tm == 0 and n % tn == 0
    return pl.pallas_call(
        functools.partial(_matmul_kernel, tn=tn),
        grid=(m // tm,),
        in_specs=[pl.BlockSpec((tm, k), lambda i: (i, 0)),
                  pl.BlockSpec((k, n), lambda i: (0, 0))],
        out_specs=pl.BlockSpec((tm, n), lambda i: (i, 0)),
        out_shape=jax.ShapeDtypeStruct((m, n), out_dtype),
        compiler_params=_cparams("parallel"),
        name=name,
    )(x, w)


def _na_bias_table(rpb, rows):
    kh = min(WIN_H, rows)
    off = jnp.arange(kh)[:, None]
    j = jnp.arange(kh)[None, :]
    ri = j - off + (WIN_H - 1)
    c = jnp.arange(GRID_W)[:, None]
    kc = jnp.arange(GRID_W)[None, :]
    cs = jnp.clip(c - WIN_W // 2, 0, GRID_W - WIN_W)
    valid = (kc >= cs) & (kc < cs + WIN_W)
    ci = kc - c + (WIN_W - 1)
    row_sel = (ri[:, :, None] == jnp.arange(2 * WIN_H - 1)).astype(F32)
    col_sel = (ci[:, :, None] == jnp.arange(2 * WIN_W - 1)).astype(F32)
    tab = jnp.einsum('hab,oja,ckb->hocjk', rpb.astype(F32), row_sel, col_sel,
                     precision=lax.Precision.HIGHEST)
    tab = jnp.where(valid[None, None, :, None, :], tab, NEG_BIG)
    tab = tab.reshape(NA_HEADS // 2, 2, kh, GRID_W, kh * GRID_W)
    tab = tab.transpose(0, 2, 1, 3, 4).reshape(NA_HEADS // 2, kh, 2 * GRID_W, kh * GRID_W)
    return tab.astype(F32)


def _na_kernel(q_ref, k_ref, v_ref, b_ref, o_ref, *, rows, kh):
    lane = lax.broadcasted_iota(jnp.int32, (GRID_W, LANE), 1)
    first = lane < HEAD_DIM
    scale = HEAD_DIM ** -0.5

    def body(r, carry):
        rs = jnp.clip(r - kh // 2, 0, rows - kh)
        off = r - rs
        q = q_ref[0, pl.ds(pl.multiple_of(r * GRID_W, GRID_W), GRID_W), :]
        kb = k_ref[0, pl.ds(pl.multiple_of(rs * GRID_W, GRID_W), kh * GRID_W), :]
        vb = v_ref[0, pl.ds(pl.multiple_of(rs * GRID_W, GRID_W), kh * GRID_W), :]
        zero = jnp.zeros_like(q)
        qq = jnp.concatenate([jnp.where(first, q, zero), jnp.where(first, zero, q)], axis=0)
        s = lax.dot_general(qq, kb, (((1,), (1,)), ((), ())), preferred_element_type=F32)
        s = s * scale + b_ref[0, off]
        m = jnp.max(s, axis=-1, keepdims=True)
        p = jnp.exp(s - m)
        l = jnp.sum(p, axis=-1, keepdims=True)
        o = jnp.dot(p.astype(BF16), vb, preferred_element_type=F32) / l
        out = jnp.where(first, o[:GRID_W], o[GRID_W:])
        o_ref[0, pl.ds(pl.multiple_of(r * GRID_W, GRID_W), GRID_W), :] = out.astype(o_ref.dtype)
        return carry

    lax.fori_loop(0, rows, body, 0, unroll=NA_ROW_UNROLL)


def _neighbourhood_attention(proj, bias_tab, batch, seq):
    rows = seq // GRID_W
    kh = min(WIN_H, rows)
    npair = NA_HEADS // 2
    kern = functools.partial(_na_kernel, rows=rows, kh=kh)
    return pl.pallas_call(
        kern,
        grid=(batch, npair),
        in_specs=[pl.BlockSpec((1, seq, LANE), lambda b, h: (b, 0, h)),
                  pl.BlockSpec((1, seq, LANE), lambda b, h: (b, 0, npair + h)),
                  pl.BlockSpec((1, seq, LANE), lambda b, h: (b, 0, 2 * npair + h)),
                  pl.BlockSpec((1, kh, 2 * GRID_W, kh * GRID_W), lambda b, h: (h, 0, 0, 0))],
        out_specs=pl.BlockSpec((1, seq, LANE), lambda b, h: (b, 0, h)),
        out_shape=jax.ShapeDtypeStruct((batch, seq, TOKEN_W), BF16),
        compiler_params=_cparams("parallel", "arbitrary"),
        name="na_attention",
    )(proj, proj, proj, bias_tab)


def _conv_silu_kernel(x_ref, w_ref, b_ref, s_ref, o_ref, *, seq):
    x = x_ref[0]
    t = lax.broadcasted_iota(jnp.int32, x.shape, 0)
    acc = jnp.zeros_like(x) + b_ref[...]
    for j in range(CONV_K):
        sh = CONV_K // 2 - j
        if sh == 0:
            xs = x
        elif sh > 0:
            xs = jnp.where(t >= sh, pltpu.roll(x, sh, 0), 0.0)
        else:
            xs = jnp.where(t < seq + sh, pltpu.roll(x, seq + sh, 0), 0.0)
        acc = acc + w_ref[j:j + 1, :] * xs
    y = acc * jax.nn.sigmoid(acc) * s_ref[...]
    o_ref[0] = y.astype(o_ref.dtype)


def _conv_silu(proj, conv_w, conv_b, col_scale, batch, seq):
    ncol = 2 * ML_W_PAD // LANE
    kern = functools.partial(_conv_silu_kernel, seq=seq)
    return pl.pallas_call(
        kern,
        grid=(batch, ncol),
        in_specs=[pl.BlockSpec((1, seq, LANE), lambda b, j: (b, 0, j)),
                  pl.BlockSpec((CONV_K, LANE), lambda b, j: (0, j)),
                  pl.BlockSpec((1, LANE), lambda b, j: (0, j)),
                  pl.BlockSpec((1, LANE), lambda b, j: (0, j))],
        out_specs=pl.BlockSpec((1, seq, LANE), lambda b, j: (b, 0, j)),
        out_shape=jax.ShapeDtypeStruct((batch, seq, 2 * ML_W_PAD), BF16),
        compiler_params=_cparams("parallel", "arbitrary"),
        name="ml_conv_silu",
    )(proj, conv_w, conv_b, col_scale)


def _log_sigmoid(x):
    return jnp.minimum(x, 0.0) - jnp.log1p(jnp.exp(-jnp.abs(x)))


def _mlstm_intra_kernel(q_ref, k_ref, v_ref, g_ref, gb_ref,
                        af_ref, ab_ref, kwf_ref, kwb_ref, vx_ref, scf_ref, scb_ref, *, chunks):
    L = ML_CHUNK
    row = lax.broadcasted_iota(jnp.int32, (L, L), 0)
    col = lax.broadcasted_iota(jnp.int32, (L, L), 1)
    hlane = lax.broadcasted_iota(jnp.int32, (L, ML_DH_PAD), 1)
    glane = lax.broadcasted_iota(jnp.int32, (L, LANE), 1)
    is_f = (glane >= 2 * ML_HEADS) & (glane < 4 * ML_HEADS)
    dirs = ((col <= row, L - 1, af_ref, kwf_ref, scf_ref),
            (col >= row, 0, ab_ref, kwb_ref, scb_ref))

    for c in range(chunks):
        rows = slice(c * L, (c + 1) * L)
        g = g_ref[0, rows, :] + gb_ref[...]
        act = jnp.where(is_f, _log_sigmoid(g), g)
        act_t = act.T
        cums = []
        for valid, _, _, _, _ in dirs:
            bcum = jnp.dot(valid.astype(F32), act, preferred_element_type=F32,
                           precision=lax.Precision.HIGHEST)
            cums.append((bcum, bcum.T))
        scs = [jnp.zeros((L, LANE), F32), jnp.zeros((L, LANE), F32)]

        for h in range(ML_HEADS):
            sl = slice(h * ML_DH_PAD, (h + 1) * ML_DH_PAD)
            qh = q_ref[0, rows, sl]
            kh = k_ref[0, rows, sl]
            vh = jnp.where(hlane == ML_DH, 1.0, v_ref[0, rows, sl]).astype(BF16)
            vx_ref[0, rows, sl] = vh
            s_raw = lax.dot_general(qh, kh, (((1,), (1,)), ((), ())), preferred_element_type=F32)
            kf = kh.astype(F32)
            for d, (valid, last, a_ref, kw_ref, _) in enumerate(dirs):
                bcum, bcum_t = cums[d]
                li_lane = d * ML_HEADS + h
                lf_lane = 2 * ML_HEADS + d * ML_HEADS + h
                b_col = bcum[:, lf_lane:lf_lane + 1]
                b_row = bcum_t[lf_lane:lf_lane + 1, :]
                li_col = act[:, li_lane:li_lane + 1]
                li_row = act_t[li_lane:li_lane + 1, :]
                dm = jnp.where(valid, b_col - b_row + li_row, NEG_BIG)
                mloc = jnp.max(dm, axis=-1, keepdims=True)
                p = (s_raw * jnp.exp(dm - mloc)).astype(BF16)
                a_ref[0, rows, sl] = jnp.dot(p, vh, preferred_element_type=F32)
                b_last = b_col[last:last + 1, :]
                cloc = mloc[last:last + 1, :]
                w_src = jnp.exp(b_last - b_col + li_col - cloc)
                kw_ref[0, rows, sl] = (kf * w_src).astype(BF16)
                scs[d] = jnp.where(glane == h, b_col, jnp.where(glane == ML_HEADS + h, mloc, scs[d]))
        scf_ref[0, rows, :] = scs[0]
        scb_ref[0, rows, :] = scs[1]


def _mlstm_inter_kernel(qf_ref, af_ref, kwf_ref, vxf_ref, scf_ref, qb_ref, ab_ref, kwb_ref, vxb_ref, scb_ref,
                        of_ref, ob_ref, c_ref, m_ref):
    L = ML_CHUNK

    @pl.when(pl.program_id(1) == 0)
    def _():
        c_ref[...] = jnp.zeros_like(c_ref)
        m_ref[...] = jnp.zeros_like(m_ref)

    hlane = lax.broadcasted_iota(jnp.int32, (L, ML_DH_PAD), 1)
    dirs = ((L - 1, qf_ref, af_ref, kwf_ref, vxf_ref, scf_ref, of_ref),
            (0, qb_ref, ab_ref, kwb_ref, vxb_ref, scb_ref, ob_ref))
    for d, (last, q_ref, a_ref, kw_ref, vx_ref, sc_ref, o_ref) in enumerate(dirs):
        sc = sc_ref[0]
        for h in range(ML_HEADS):
            sl = slice(h * ML_DH_PAD, (h + 1) * ML_DH_PAD)
            b_col = sc[:, h:h + 1]
            mloc = sc[:, ML_HEADS + h:ML_HEADS + h + 1]
            m_prev = m_ref[d, h][0:1, 0:1]
            inter = b_col + m_prev
            m_t = jnp.maximum(inter, mloc)
            c_old = c_ref[d, h]
            num = (jnp.exp(inter - m_t) * jnp.dot(q_ref[0, :, sl], c_old.astype(BF16), preferred_element_type=F32)
                   + jnp.exp(mloc - m_t) * a_ref[0, :, sl])
            den = num[:, ML_DH:ML_DH + 1]
            hout = num / jnp.maximum(jnp.abs(den), jnp.exp(-m_t))
            o_ref[0, :, sl] = jnp.where(hlane < ML_DH, hout, 0.0)

            b_last = b_col[last:last + 1, :]
            cloc = mloc[last:last + 1, :]
            m_new = m_t[last:last + 1, :]
            upd = lax.dot_general(kw_ref[0, :, sl], vx_ref[0, :, sl], (((0,), (0,)), ((), ())),
                                  preferred_element_type=F32)
            c_ref[d, h] = jnp.exp(b_last + m_prev - m_new) * c_old + jnp.exp(cloc - m_new) * upd
            m_ref[d, h] = jnp.zeros(m_ref.shape[2:], F32) + m_new


def _mlstm(qk, proj, gate_b, batch, seq):
    nc = seq // ML_CHUNK
    chunks = ML_INTRA_CHUNKS
    assert nc % chunks == 0
    tb = chunks * ML_CHUNK
    gcol = (4 * ML_W_PAD + MEM_W) // LANE
    at = lambda col: (lambda b, c: (b, c, col))
    f32w = jax.ShapeDtypeStruct((batch, seq, ML_W_PAD), F32)
    bf16w = jax.ShapeDtypeStruct((batch, seq, ML_W_PAD), BF16)
    scal = jax.ShapeDtypeStruct((batch, seq, LANE), F32)
    wide_p = pl.BlockSpec((1, tb, ML_W_PAD), at(0))
    lane_p = pl.BlockSpec((1, tb, LANE), at(0))
    a_f, a_b, kw_f, kw_b, vx, sc_f, sc_b = pl.pallas_call(
        functools.partial(_mlstm_intra_kernel, chunks=chunks),
        grid=(batch, nc // chunks),
        in_specs=[pl.BlockSpec((1, tb, ML_W_PAD), at(0)),
                  pl.BlockSpec((1, tb, ML_W_PAD), at(1)),
                  pl.BlockSpec((1, tb, ML_W_PAD), at(2)),
                  pl.BlockSpec((1, tb, LANE), at(gcol)),
                  pl.BlockSpec((1, LANE), lambda b, c: (0, 0))],
        out_specs=[wide_p, wide_p, wide_p, wide_p, wide_p, lane_p, lane_p],
        out_shape=[f32w, f32w, bf16w, bf16w, bf16w, scal, scal],
        compiler_params=_cparams("parallel", "parallel"),
        name="mlstm_intra",
    )(qk, qk, proj, proj, gate_b)

    fwd = lambda b, c: (b, c, 0)
    bwd = lambda b, c: (b, nc - 1 - c, 0)
    wide = lambda m: pl.BlockSpec((1, ML_CHUNK, ML_W_PAD), m)
    lane = lambda m: pl.BlockSpec((1, ML_CHUNK, LANE), m)
    return pl.pallas_call(
        _mlstm_inter_kernel,
        grid=(batch, nc),
        in_specs=[wide(fwd), wide(fwd), wide(fwd), wide(fwd), lane(fwd),
                  wide(bwd), wide(bwd), wide(bwd), wide(bwd), lane(bwd)],
        out_specs=[wide(fwd), wide(bwd)],
        out_shape=[f32w, f32w],
        scratch_shapes=[pltpu.VMEM((2, ML_HEADS, ML_DH_PAD, ML_DH_PAD), F32),
                        pltpu.VMEM((2, ML_HEADS, 8, LANE), F32)],
        compiler_params=_cparams("parallel", "arbitrary"),
        name="mlstm_inter",
    )(qk, a_f, kw_f, vx, sc_f, qk, a_b, kw_b, vx, sc_b)


def _layer_norm(y, g, b):
    mu = jnp.mean(y, axis=-1, keepdims=True)
    yc = y - mu
    var = jnp.mean(yc * yc, axis=-1, keepdims=True)
    return yc * lax.rsqrt(var + LN_EPS) * g + b


def _first_argmax(val, lane):
    mx = jnp.max(val, axis=-1, keepdims=True)
    idx = jnp.min(jnp.where(val == mx, lane.astype(F32), float(LANE)), axis=-1, keepdims=True)
    return mx, idx.astype(jnp.int32)


def _route(x1, wr_ref, br_ref):
    x_hi = x1.astype(BF16)
    x_lo = (x1 - x_hi.astype(F32)).astype(BF16)
    t = jnp.dot(x_hi, wr_ref[...], preferred_element_type=F32)
    logits = (t[:, :LANE] + t[:, LANE:] + jnp.dot(x_lo, wr_ref[:, :LANE], preferred_element_type=F32)
              + br_ref[...])
    lane = lax.broadcasted_iota(jnp.int32, logits.shape, 1)
    is_g = lane < N_GROUPS
    gl = jnp.where(is_g, logits, NEG_BIG)
    gmax, gidx = _first_argmax(gl, lane)
    gsum = jnp.sum(jnp.where(is_g, jnp.exp(gl - gmax), 0.0), axis=-1, keepdims=True)
    g_gate = 1.0 / gsum
    lo = N_GROUPS + gidx * EXPERTS_PER_GROUP
    in_grp = (lane >= lo) & (lane < lo + EXPERTS_PER_GROUP)
    el = jnp.where(in_grp, logits, NEG_BIG)
    emax = jnp.max(el, axis=-1, keepdims=True)
    pe = jnp.where(in_grp, jnp.exp(el - emax), 0.0)
    prob = pe / jnp.sum(pe, axis=-1, keepdims=True)
    prob = jnp.where(in_grp, prob, -1.0)
    p1, i1 = _first_argmax(prob, lane)
    p2, i2 = _first_argmax(jnp.where(lane == i1, -1.0, prob), lane)
    tot = p1 + p2
    return i1 - N_GROUPS, i2 - N_GROUPS, g_gate * p1 / tot, g_gate * p2 / tot


def _post_kernel(*refs, ml):
    if ml:
        (hf_ref, hb_ref, op_ref, ng_ref, qm_ref, x_ref, mkv_ref, wo_ref, g_ref, b_ref, wr_ref, br_ref,
         x1_ref, ri_ref, rf_ref, cnt_ref) = refs
        hsum = hf_ref[...] + hb_ref[...]
        hl = lax.broadcasted_iota(jnp.int32, (hsum.shape[0], ML_DH_PAD), 1)
        parts = []
        for h in range(ML_HEADS):
            hh = hsum[:, h * ML_DH_PAD:(h + 1) * ML_DH_PAD]
            mu = jnp.sum(hh, axis=-1, keepdims=True) * (1.0 / ML_DH)
            hc = jnp.where(hl < ML_DH, hh - mu, 0.0)
            var = jnp.sum(hc * hc, axis=-1, keepdims=True) * (1.0 / ML_DH)
            parts.append(hc * lax.rsqrt(var + LN_EPS))
        hn = jnp.concatenate(parts, axis=-1) * ng_ref[...]
        tok = (jax.nn.sigmoid(op_ref[...]) * hn).astype(BF16)
    else:
        (tok_ref, qm_ref, x_ref, mkv_ref, wo_ref, g_ref, b_ref, wr_ref, br_ref,
         x1_ref, ri_ref, rf_ref, cnt_ref) = refs
        tok = tok_ref[...]

    qm = qm_ref[...].astype(BF16)
    mk = mkv_ref[0, :, 0:MEM_W]
    mv = mkv_ref[0, :, MEM_W:2 * MEM_W]
    mlane = lax.broadcasted_iota(jnp.int32, qm.shape, 1)
    matt = jnp.zeros(qm.shape, F32)
    for h in range(MEM_HEADS):
        hm = (mlane >= h * MEM_DH) & (mlane < (h + 1) * MEM_DH)
        qh = jnp.where(hm, qm, jnp.zeros_like(qm))
        s = lax.dot_general(qh, mk, (((1,), (1,)), ((), ())), preferred_element_type=F32) * (MEM_DH ** -0.5)
        p = jnp.exp(s - jnp.max(s, axis=-1, keepdims=True))
        l = jnp.sum(p, axis=-1, keepdims=True)
        o = jnp.dot(p.astype(BF16), mv, preferred_element_type=F32) / l
        matt = jnp.where(hm, o, matt)

    mixed = jnp.concatenate([tok, matt.astype(BF16)], axis=-1)
    y = ALPHA * x_ref[...] + jnp.dot(mixed, wo_ref[...], preferred_element_type=F32)
    x1 = _layer_norm(y, g_ref[...], b_ref[...])
    x1_ref[...] = x1

    e1, e2, g1, g2 = _route(x1, wr_ref, br_ref)
    lane = lax.broadcasted_iota(jnp.int32, ri_ref.shape, 1)
    tm = x1.shape[0]
    oh1 = jnp.where(lane == e1, 1.0, 0.0)
    oh2 = jnp.where(lane == e2, 1.0, 0.0)
    trow = lax.broadcasted_iota(jnp.int32, (tm, tm), 0)
    tcol = lax.broadcasted_iota(jnp.int32, (tm, tm), 1)
    earlier = jnp.where(tcol < trow, 1.0, 0.0).astype(BF16)
    before1 = jnp.dot(earlier, oh1.astype(BF16), preferred_element_type=F32)
    before2 = jnp.dot(earlier, oh2.astype(BF16), preferred_element_type=F32)
    c1 = jnp.sum(oh1, axis=0, keepdims=True)
    c2 = jnp.sum(oh2, axis=0, keepdims=True)
    rank1 = jnp.sum(oh1 * before1, axis=-1, keepdims=True).astype(jnp.int32)
    rank2 = jnp.sum(oh2 * (before2 + c1), axis=-1, keepdims=True).astype(jnp.int32)
    ri_ref[...] = jnp.where(lane == 0, e1, jnp.where(lane == 1, e2,
                            jnp.where(lane == 2, rank1, jnp.where(lane == 3, rank2, 0))))
    rf_ref[...] = jnp.where(lane == 0, g1, jnp.where(lane == 1, g2, 0.0))
    cnt_ref[...] = jnp.broadcast_to(c1 + c2, cnt_ref.shape).astype(jnp.int32)


def _post_mixer(mix_inputs, x, mem_kv, w_out, ln_g, ln_b, w_route, b_route, seq, ml):
    n = x.shape[0]
    tm = POST_TM
    tiles_per_batch = seq // tm
    row = lambda i: (i, 0)
    const = lambda i: (0, 0)
    if ml:
        h_f, h_b, proj, norm_g = mix_inputs
        ins = [h_f, h_b, proj, norm_g, proj]
        specs = [pl.BlockSpec((tm, ML_W_PAD), row),
                 pl.BlockSpec((tm, ML_W_PAD), row),
                 pl.BlockSpec((tm, ML_W_PAD), lambda i: (i, 3)),
                 pl.BlockSpec((1, ML_W_PAD), const),
                 pl.BlockSpec((tm, MEM_W), lambda i: (i, 4 * ML_W_PAD // MEM_W))]
    else:
        tok, proj = mix_inputs
        ins = [tok, proj]
        specs = [pl.BlockSpec((tm, TOKEN_W), row),
                 pl.BlockSpec((tm, MEM_W), lambda i: (i, 3 * TOKEN_W // MEM_W))]
    ins += [x, mem_kv, w_out, ln_g, ln_b, w_route, b_route]
    specs += [pl.BlockSpec((tm, D_MODEL), row),
              pl.BlockSpec((1, N_MEM, 2 * MEM_W), lambda i: (i // tiles_per_batch, 0, 0)),
              pl.BlockSpec(w_out.shape, const),
              pl.BlockSpec((1, D_MODEL), const),
              pl.BlockSpec((1, D_MODEL), const),
              pl.BlockSpec((D_MODEL, 2 * LANE), const),
              pl.BlockSpec((1, LANE), const)]
    return pl.pallas_call(
        functools.partial(_post_kernel, ml=ml),
        grid=(n // tm,),
        in_specs=specs,
        out_specs=[pl.BlockSpec((tm, D_MODEL), row),
                   pl.BlockSpec((tm, LANE), row),
                   pl.BlockSpec((tm, LANE), row),
                   pl.BlockSpec((8, LANE), row)],
        out_shape=[jax.ShapeDtypeStruct((n, D_MODEL), F32),
                   jax.ShapeDtypeStruct((n, LANE), jnp.int32),
                   jax.ShapeDtypeStruct((n, LANE), F32),
                   jax.ShapeDtypeStruct((n // tm * 8, LANE), jnp.int32)],
        compiler_params=_cparams("parallel"),
        name="post_mixer_ml" if ml else "post_mixer_na",
    )(*ins)


def _start_segment_copies(tile, dst_ref, cnt_ref, loff_ref, make_copy):
    def seg(e, carry):
        idx = tile * N_EXPERTS + e
        n8 = cnt_ref[idx]
        l0 = loff_ref[idx]
        g0 = dst_ref[idx]

        def piece(b):
            size = SEG_ALIGN << b
            done = n8 & (size - 1)

            @pl.when((n8 & size) != 0)
            def _():
                make_copy(pl.multiple_of(l0 + done, SEG_ALIGN), pl.multiple_of(g0 + done, SEG_ALIGN), size).start()

        for b in range(SEG_SMALL_BITS):
            piece(b)

        @pl.when(n8 >= (SEG_ALIGN << SEG_SMALL_BITS))
        def _():
            for b in range(SEG_SMALL_BITS, SEG_BITS):
                piece(b)
        return carry

    lax.fori_loop(0, N_EXPERTS, seg, 0)


def _wait_rows(total, make_desc):
    for b in range((LBUF // SEG_ALIGN).bit_length()):
        size = SEG_ALIGN << b

        @pl.when((total & size) != 0)
        def _():
            make_desc(size).wait()


def _pack_bf16_pairs(x, rounded):
    if not rounded:
        x = x.astype(BF16).astype(F32)
    lo = lax.bitcast_convert_type(x[:, :PACKED_W], jnp.uint32) >> 16
    hi = lax.bitcast_convert_type(x[:, PACKED_W:], jnp.uint32) & jnp.uint32(0xFFFF0000)
    return lo | hi


def _unpack_bf16_pairs(u):
    lo = lax.bitcast_convert_type(u << 16, F32)
    hi = lax.bitcast_convert_type(u & jnp.uint32(0xFFFF0000), F32)
    return jnp.concatenate([lo, hi], axis=-1).astype(BF16)


def _local_positions(ri, lrow):
    lane = lax.broadcasted_iota(jnp.int32, ri.shape, 1)
    pos = []
    for k in range(TOP_K):
        off = jnp.sum(jnp.where(lane == ri[:, k:k + 1], lrow, 0).astype(F32), axis=-1, keepdims=True)
        pos.append(off.astype(jnp.int32) + ri[:, TOP_K + k:TOP_K + k + 1])
    return pos


def _dispatch_kernel(dst_ref, cnt_ref, loff_ref, tot_ref, zdst_ref, zcnt_ref, nused_ref,
                     x_ref, ri_ref, lrow_ref, xs_ref, sbuf, zbuf, sems):
    t = pl.program_id(0)
    cur = lax.rem(t, 2)
    tm = x_ref.shape[0]
    p0, p1 = _local_positions(ri_ref[...], lrow_ref[0, 0:1, :])
    lane = lax.broadcasted_iota(jnp.int32, (tm, LANE), 1)
    both = jnp.where(lane == 0, p0, jnp.where(lane == 1, p1, -1)).astype(F32)
    both_t = both.T.astype(jnp.int32)
    row = lax.broadcasted_iota(jnp.int32, (LBUF, tm), 0)
    perm = jnp.where((row == both_t[0:1, :]) | (row == both_t[1:2, :]), 1.0, 0.0).astype(BF16)
    sbuf[cur] = _pack_bf16_pairs(jnp.dot(perm, x_ref[...].astype(BF16), preferred_element_type=F32), rounded=True)

    def seg_copy(slot):
        def make(lrow, grow, n):
            return pltpu.make_async_copy(sbuf.at[slot, pl.ds(lrow, n)], xs_ref.at[pl.ds(grow, n)], sems.at[slot])
        return make

    _start_segment_copies(t, dst_ref, cnt_ref, loff_ref, seg_copy(cur))

    @pl.when(t > 0)
    def _():
        _wait_rows(tot_ref[t - 1], lambda n: seg_copy(1 - cur)(0, 0, n))

    @pl.when(t == pl.num_programs(0) - 1)
    def _():
        _wait_rows(tot_ref[t], lambda n: seg_copy(cur)(0, 0, n))
        sem = sems.at[0]
        zbuf[...] = jnp.zeros_like(zbuf)
        for wait in (False, True):
            def fill(e, carry):
                n8 = zcnt_ref[e]
                g0 = zdst_ref[e]
                for b in range((MOE_BLK // SEG_ALIGN).bit_length() - 1):
                    size = SEG_ALIGN << b
                    done = n8 & (size - 1)

                    @pl.when((n8 & size) != 0)
                    def _():
                        cp = pltpu.make_async_copy(zbuf.at[pl.ds(0, size)],
                                                   xs_ref.at[pl.ds(pl.multiple_of(g0 + done, SEG_ALIGN), size)], sem)
                        if wait:
                            cp.wait()
                        else:
                            cp.start()
                return carry
            lax.fori_loop(0, N_EXPERTS, fill, 0)

            def fill_tail(h, carry):
                cp = pltpu.make_async_copy(zbuf, xs_ref.at[pl.ds(pl.multiple_of(h * zbuf.shape[0], SEG_ALIGN),
                                                                 zbuf.shape[0])], sem)
                if wait:
                    cp.wait()
                else:
                    cp.start()
                return carry
            halves = MOE_BLK // zbuf.shape[0]
            lax.fori_loop(nused_ref[0] * halves, xs_ref.shape[0] // zbuf.shape[0], fill_tail, 0)


def _dispatch(plan, x1, route_i, n_rows):
    n = x1.shape[0]
    tm = POST_TM
    row = lambda i, *_: (i, 0)
    return pl.pallas_call(
        _dispatch_kernel,
        grid_spec=pltpu.PrefetchScalarGridSpec(
            num_scalar_prefetch=7,
            grid=(n // tm,),
            in_specs=[pl.BlockSpec((tm, D_MODEL), row),
                      pl.BlockSpec((tm, LANE), row),
                      pl.BlockSpec((1, 8, LANE), lambda i, *_: (i, 0, 0)),
                      ],
            out_specs=pl.BlockSpec(memory_space=pl.ANY),
            scratch_shapes=[pltpu.VMEM((2, LBUF, PACKED_W), jnp.uint32),
                            pltpu.VMEM((MOE_BLK // 2, PACKED_W), jnp.uint32),
                            pltpu.SemaphoreType.DMA((2,))]),
        out_shape=jax.ShapeDtypeStruct((n_rows, PACKED_W), jnp.uint32),
        compiler_params=_cparams("arbitrary"),
        name="moe_dispatch",
    )(plan["seg_dst"], plan["seg_cnt"], plan["seg_loff"], plan["seg_tot"], plan["fill_dst"], plan["fill_cnt"],
      plan["n_used"], x1, route_i, plan["loff_rows"])


def _expert_kernel(beid_ref, nused_ref, x_ref, w1_ref, w3_ref, w2_ref, o_ref, w1b, w3b, w2b):
    i = pl.program_id(0)
    prev = beid_ref[jnp.maximum(i - 1, 0)]
    fresh = (i == 0) | (beid_ref[i] != prev)

    @pl.when(fresh)
    def _():
        w1b[...] = w1_ref[...].astype(BF16)
        w3b[...] = w3_ref[...].astype(BF16)
        w2b[...] = w2_ref[...].astype(BF16)

    @pl.when(i < nused_ref[0])
    def _():
        x = _unpack_bf16_pairs(x_ref[...])
        a = jnp.dot(x, w1b[...], preferred_element_type=F32)
        b = jnp.dot(x, w3b[...], preferred_element_type=F32)
        hmid = (a * jax.nn.sigmoid(a) * b).astype(BF16)
        o_ref[...] = _pack_bf16_pairs(jnp.dot(hmid, w2b[...], preferred_element_type=F32), rounded=False)

    @pl.when(i >= nused_ref[0])
    def _():
        o_ref[...] = jnp.zeros_like(o_ref)


def _expert_mlp(block_eid, n_used, xs, w1, w3, w2, layer):
    p = xs.shape[0]
    nb = p // MOE_BLK
    wmap13 = lambda i, be, nu: (layer, be[i], 0, 0)
    return pl.pallas_call(
        _expert_kernel,
        grid_spec=pltpu.PrefetchScalarGridSpec(
            num_scalar_prefetch=2,
            grid=(nb,),
            in_specs=[pl.BlockSpec((MOE_BLK, PACKED_W), lambda i, be, nu: (jnp.maximum(jnp.minimum(i, nu[0] - 1), 0), 0)),
                      pl.BlockSpec((None, None, D_MODEL, D_EXPERT), wmap13),
                      pl.BlockSpec((None, None, D_MODEL, D_EXPERT), wmap13),
                      pl.BlockSpec((None, None, D_EXPERT, D_MODEL), wmap13)],
            out_specs=pl.BlockSpec((MOE_BLK, PACKED_W), lambda i, be, nu: (i, 0)),
            scratch_shapes=[pltpu.VMEM((D_MODEL, D_EXPERT), BF16),
                            pltpu.VMEM((D_MODEL, D_EXPERT), BF16),
                            pltpu.VMEM((D_EXPERT, D_MODEL), BF16)]),
        out_shape=jax.ShapeDtypeStruct((p, PACKED_W), jnp.uint32),
        compiler_params=_cparams("arbitrary"),
        name="moe_experts",
    )(block_eid, n_used, xs, w1, w3, w2)


def _combine_kernel(dst_ref, cnt_ref, loff_ref, tot_ref, x_ref, ri_ref, rf_ref, lrow_ref, g_ref, b_ref, ys_ref,
                    o_ref, ybuf, sems):
    t = pl.program_id(0)
    cur = lax.rem(t, 2)

    def seg_copy(slot):
        def make(lrow, grow, n):
            return pltpu.make_async_copy(ys_ref.at[pl.ds(grow, n)], ybuf.at[slot, pl.ds(lrow, n)], sems.at[slot])
        return make

    @pl.when(t == 0)
    def _():
        ybuf[...] = jnp.zeros_like(ybuf)
        _start_segment_copies(t, dst_ref, cnt_ref, loff_ref, seg_copy(cur))

    @pl.when(t + 1 < pl.num_programs(0))
    def _():
        _start_segment_copies(t + 1, dst_ref, cnt_ref, loff_ref, seg_copy(1 - cur))

    p0, p1 = _local_positions(ri_ref[...], lrow_ref[0, 0:1, :])
    col = lax.broadcasted_iota(jnp.int32, (x_ref.shape[0], LBUF), 1)
    sel0 = jnp.where(col == p0, 1.0, 0.0).astype(BF16)
    sel1 = jnp.where(col == p1, 1.0, 0.0).astype(BF16)
    _wait_rows(tot_ref[t], lambda n: seg_copy(cur)(0, 0, n))

    yb = _unpack_bf16_pairs(ybuf[cur])
    y0 = jnp.dot(sel0, yb, preferred_element_type=F32)
    y1 = jnp.dot(sel1, yb, preferred_element_type=F32)
    ffn = y0 * rf_ref[:, 0:1] + y1 * rf_ref[:, 1:2]
    o_ref[...] = _layer_norm(ALPHA * x_ref[...] + ffn, g_ref[...], b_ref[...])


def _combine_norm(plan, x1, route_i, route_f, ys, ln_g, ln_b):
    n = x1.shape[0]
    tm = POST_TM
    row = lambda i, *_: (i, 0)
    const = lambda i, *_: (0, 0)
    return pl.pallas_call(
        _combine_kernel,
        grid_spec=pltpu.PrefetchScalarGridSpec(
            num_scalar_prefetch=4,
            grid=(n // tm,),
            in_specs=[pl.BlockSpec((tm, D_MODEL), row),
                      pl.BlockSpec((tm, LANE), row),
                      pl.BlockSpec((tm, LANE), row),
                      pl.BlockSpec((1, 8, LANE), lambda i, *_: (i, 0, 0)),
                      pl.BlockSpec((1, D_MODEL), const),
                      pl.BlockSpec((1, D_MODEL), const),
                      pl.BlockSpec(memory_space=pl.ANY)],
            out_specs=pl.BlockSpec((tm, D_MODEL), row),
            scratch_shapes=[pltpu.VMEM((2, LBUF, PACKED_W), jnp.uint32),
                            pltpu.SemaphoreType.DMA((2,))]),
        out_shape=jax.ShapeDtypeStruct((n, D_MODEL), F32),
        compiler_params=_cparams("arbitrary"),
        name="moe_combine_norm",
    )(plan["seg_dst"], plan["seg_cnt"], plan["seg_loff"], plan["seg_tot"], x1, route_i, route_f,
      plan["loff_rows"], ln_g, ln_b, ys)


def _routing_plan(tile_counts, n_tokens):
    n_tiles = tile_counts.shape[0]
    e_before = jnp.arange(N_EXPERTS)[:, None] < jnp.arange(N_EXPERTS)[None, :]
    t_before = jnp.arange(n_tiles)[:, None] < jnp.arange(n_tiles)[None, :]
    cnt8 = (tile_counts + SEG_ALIGN - 1) // SEG_ALIGN * SEG_ALIGN
    loff = jnp.sum(jnp.where(e_before[None], cnt8[:, :, None], 0), axis=1)
    tot8 = jnp.sum(cnt8, axis=0)
    padded = (tot8 + MOE_BLK - 1) // MOE_BLK * MOE_BLK
    pad_start = jnp.sum(jnp.where(e_before, padded[:, None], 0), axis=0)
    pad_end = pad_start + padded
    seg_dst = pad_start[None, :] + jnp.sum(jnp.where(t_before[:, :, None], cnt8[:, None, :], 0), axis=0)
    worst_rows = n_tokens * TOP_K + n_tiles * N_EXPERTS * (SEG_ALIGN - 1) + N_EXPERTS * (MOE_BLK - 1)
    n_blocks = -(-worst_rows // MOE_BLK)
    block_row = jnp.arange(n_blocks, dtype=jnp.int32) * MOE_BLK
    block_eid = jnp.minimum(jnp.sum(pad_end[None, :] <= block_row[:, None], axis=1), N_EXPERTS - 1).astype(jnp.int32)
    i32 = lambda v: v.astype(jnp.int32)
    loff_rows = jnp.broadcast_to(jnp.pad(i32(loff), ((0, 0), (0, LANE - N_EXPERTS)))[:, None, :],
                                 (n_tiles, 8, LANE))
    return dict(seg_dst=i32(seg_dst).reshape(-1), seg_cnt=i32(cnt8).reshape(-1), seg_loff=i32(loff).reshape(-1),
                seg_tot=i32(jnp.sum(cnt8, axis=1)),
                fill_dst=i32(pad_start + tot8), fill_cnt=i32(padded - tot8), loff_rows=loff_rows,
                block_eid=block_eid, n_used=i32(pad_end[-1] // MOE_BLK).reshape(1), n_rows=n_blocks * MOE_BLK)


def _moe(x1, route_i, route_f, tile_cnt, w1, w3, w2, ln_g, ln_b, layer):
    n = x1.shape[0]
    counts = tile_cnt.reshape(n // POST_TM, 8, LANE)[:, 0, :N_EXPERTS]
    plan = _routing_plan(counts, n)
    xs = _dispatch(plan, x1, route_i, plan["n_rows"])
    ys = _expert_mlp(plan["block_eid"], plan["n_used"], xs, w1, w3, w2, layer)
    return _combine_norm(plan, x1, route_i, route_f, ys, ln_g, ln_b)


def _pad_heads(w):
    lead = w.shape[:-1]
    w = w.reshape(*lead, ML_HEADS, ML_DH)
    w = jnp.pad(w, [(0, 0)] * len(lead) + [(0, 0), (0, ML_DH_PAD - ML_DH)])
    return w.reshape(*lead, ML_W_PAD)


def _ml_in_weight(w):
    tw = TOKEN_W
    segs = [_pad_heads(w[:, i * tw:(i + 1) * tw]) for i in range(4)]
    gates = w[:, 4 * tw:4 * tw + 4 * ML_HEADS]
    qmem = w[:, 4 * tw + 4 * ML_HEADS:]
    gates = jnp.pad(gates, ((0, 0), (0, LANE - 4 * ML_HEADS)))
    return jnp.concatenate(segs + [qmem, gates], axis=1)


def _route_weight(wg, bg, we, be):
    w = jnp.concatenate([wg, we], axis=1)
    b = jnp.concatenate([bg, be], axis=0)
    pad = LANE - w.shape[1]
    w = jnp.pad(w, ((0, 0), (0, pad))).astype(F32)
    w_hi = w.astype(BF16)
    w_lo = (w - w_hi.astype(F32)).astype(BF16)
    return jnp.concatenate([w_hi, w_lo], axis=1), jnp.pad(b, (0, pad)).reshape(1, LANE).astype(F32)


def kernel(x, mem, w_mem_kv, na_w_in, na_rpb, ml_w_in, ml_conv_w, ml_conv_b, ml_gate_b, ml_norm_g, w_out,
           ln1_g, ln1_b, ln2_g, ln2_b, moe_wg, moe_bg, moe_we, moe_be, moe_w1, moe_w3, moe_w2):
    batch, seq, d = x.shape
    n = batch * seq
    rows = seq // GRID_W
    xf = x.reshape(n, d)

    mem_kv = _matmul(mem.reshape(batch * N_MEM, d), w_mem_kv.astype(BF16), BF16,
                     min(PROJ_TM, batch * N_MEM), 2 * MEM_W, "mem_kv").reshape(batch, N_MEM, 2 * MEM_W)

    for layer in range(DEPTH):
        j = layer // 2
        w_route, b_route = _route_weight(moe_wg[layer], moe_bg[layer], moe_we[layer], moe_be[layer])
        g1 = ln1_g[layer].reshape(1, d)
        b1 = ln1_b[layer].reshape(1, d)
        if layer % 2 == 0:
            proj = _matmul(xf, na_w_in[j].astype(BF16), BF16, PROJ_TM, PROJ_TN, "na_in_proj")
            tok = _neighbourhood_attention(proj.reshape(batch, seq, -1), _na_bias_table(na_rpb[j], rows),
                                           batch, seq)
            x1, route_i, route_f, tile_cnt = _post_mixer((tok.reshape(n, TOKEN_W), proj), xf, mem_kv,
                                               w_out[layer].astype(BF16), g1, b1, w_route, b_route, seq, ml=False)
        else:
            proj = _matmul(xf, _ml_in_weight(ml_w_in[j]).astype(BF16), F32, PROJ_TM, PROJ_TN, "ml_in_proj")
            proj3 = proj.reshape(batch, seq, -1)
            conv_w = jnp.concatenate([_pad_heads(ml_conv_w[j][:, :TOKEN_W]),
                                      _pad_heads(ml_conv_w[j][:, TOKEN_W:])], axis=1)
            conv_b = jnp.concatenate([_pad_heads(ml_conv_b[j][:TOKEN_W]),
                                      _pad_heads(ml_conv_b[j][TOKEN_W:])]).reshape(1, 2 * ML_W_PAD)
            col_scale = jnp.concatenate([jnp.full((ML_W_PAD,), ML_DH ** -0.5, F32),
                                         jnp.ones((ML_W_PAD,), F32)]).reshape(1, 2 * ML_W_PAD)
            qk = _conv_silu(proj3, conv_w, conv_b, col_scale, batch, seq)
            gate_b = jnp.pad(ml_gate_b[j].reshape(-1), (0, LANE - 4 * ML_HEADS)).reshape(1, LANE)
            h_f, h_b = _mlstm(qk, proj3, gate_b, batch, seq)
            norm_g = _pad_heads(ml_norm_g[j]).reshape(1, ML_W_PAD)
            wo = jnp.concatenate([_pad_heads(w_out[layer][:TOKEN_W].T).T, w_out[layer][TOKEN_W:]], axis=0)
            x1, route_i, route_f, tile_cnt = _post_mixer((h_f.reshape(n, ML_W_PAD), h_b.reshape(n, ML_W_PAD), proj, norm_g),
                                               xf, mem_kv, wo.astype(BF16), g1, b1, w_route, b_route, seq, ml=True)
        xf = _moe(x1, route_i, route_f, tile_cnt, moe_w1, moe_w3, moe_w2,
                  ln2_g[layer].reshape(1, d), ln2_b[layer].reshape(1, d), layer)
    return xf.reshape(batch, seq, d)
```

```python
import functools
import math

import jax
import jax.numpy as jnp
from jax import lax
from jax.experimental import pallas as pl
from jax.experimental.pallas import tpu as pltpu

D_MODEL = 1024
DEPTH = 4
N_MEM = 256
GRID_W = 64
TOKEN_W = 768
MEM_W = 256
HEAD_DIM = 64
NA_HEADS = 12
WIN_H = 8
WIN_W = 16
ML_HEADS = 4
ML_DH = 192
ML_CHUNK = 256
CONV_K = 5
MEM_HEADS = 4
MEM_DH = 64
N_GROUPS = 4
EXPERTS_PER_GROUP = 8
N_EXPERTS = 32
TOP_K = 2
D_EXPERT = 512
LN_EPS = 1e-5
ALPHA = (2 * DEPTH) ** 0.25

LANE = 128
ML_DH_PAD = 256
ML_W_PAD = ML_HEADS * ML_DH_PAD
VMEM_LIMIT_BYTES = 56 * 1024 * 1024

PROJ_TM = 512
PROJ_TN = 640
NA_GROUP = 4
NA_GROUP_UNROLL = 8
POST_TM = 512
MOE_BLK = 512
ML_INTRA_CHUNKS = 1
SEG_ALIGN = 8
SEG_BITS = (TOP_K * POST_TM // SEG_ALIGN).bit_length()
SEG_SMALL_BITS = 4
PACKED_W = D_MODEL // 2
XS_W = PACKED_W + LANE
LBUF = -(-(TOP_K * POST_TM + N_EXPERTS * (SEG_ALIGN - 1)) // 256) * 256
NEG_BIG = -1e30
LOG2_E = math.log2(math.e)

F32 = jnp.float32
BF16 = jnp.bfloat16


def _cparams(*sem):
    return pltpu.CompilerParams(dimension_semantics=sem, vmem_limit_bytes=VMEM_LIMIT_BYTES)


def _matmul_kernel(x_ref, w_ref, o_ref, *, tn):
    x = x_ref[...].astype(BF16)
    for j in range(w_ref.shape[1] // tn):
        cols = slice(j * tn, (j + 1) * tn)
        o_ref[:, cols] = jnp.dot(x, w_ref[:, cols], preferred_element_type=F32).astype(o_ref.dtype)


def _matmul(x, w, out_dtype, tm, tn, name):
    m, k = x.shape
    n = w.shape[1]
    assert m % tm == 0 and n % tn == 0
    return pl.pallas_call(
        functools.partial(_matmul_kernel, tn=tn),
        grid=(m // tm,),
        in_specs=[pl.BlockSpec((tm, k), lambda i: (i, 0)),
                  pl.BlockSpec((k, n), lambda i: (0, 0))],
        out_specs=pl.BlockSpec((tm, n), lambda i: (i, 0)),
        out_shape=jax.ShapeDtypeStruct((m, n), out_dtype),
        compiler_params=_cparams("parallel"),
        name=name,
    )(x, w)


def _na_bias_table(rpb, rows):
    kh, band, n_groups = _na_geometry(rows)
    g_rows = NA_GROUP
    keys, variant_of_group = [], []
    for g in range(n_groups):
        r0 = g * g_rows
        bs = min(max(r0 - kh // 2, 0), rows - band)
        key = tuple((r0 + i - bs, min(max(r0 + i - kh // 2, 0), rows - kh) - bs) for i in range(g_rows))
        if key not in keys:
            keys.append(key)
        variant_of_group.append(keys.index(key))
    geo = jnp.asarray(keys, jnp.int32)
    qrow, wstart = geo[:, :, 0:1], geo[:, :, 1:2]
    j = jnp.arange(band)[None, None, :]
    row_ok = (j >= wstart) & (j < wstart + kh)
    ri = j - qrow + (WIN_H - 1)
    c = jnp.arange(GRID_W)[:, None]
    kc = jnp.arange(GRID_W)[None, :]
    cs = jnp.clip(c - WIN_W // 2, 0, GRID_W - WIN_W)
    col_ok = (kc >= cs) & (kc < cs + WIN_W)
    ci = kc - c + (WIN_W - 1)
    row_sel = ((ri[..., None] == jnp.arange(2 * WIN_H - 1)) & row_ok[..., None]).astype(F32)
    col_sel = ((ci[:, :, None] == jnp.arange(2 * WIN_W - 1)) & col_ok[..., None]).astype(F32)
    tab = jnp.einsum('hab,vija,ckb->hvicjk', rpb.astype(F32), row_sel, col_sel,
                     precision=lax.Precision.HIGHEST)
    ok = row_ok[:, :, None, :, None] & col_ok[None, None, :, None, :]
    tab = jnp.where(ok[None], tab * LOG2_E, NEG_BIG)
    n_var = len(keys)
    tab = tab.reshape(NA_HEADS // 2, 2, n_var, g_rows * GRID_W, band * GRID_W)
    tab = tab.transpose(0, 2, 1, 3, 4).reshape(NA_HEADS // 2, n_var, 2 * g_rows * GRID_W, band * GRID_W)
    return tab.astype(F32), jnp.asarray(variant_of_group, jnp.int32)


def _na_geometry(rows):
    kh = min(WIN_H, rows)
    band = kh + NA_GROUP - 1
    band = min(band + band % 2, rows)
    assert rows % NA_GROUP == 0
    return kh, band, rows // NA_GROUP


def _na_kernel(var_ref, q_ref, k_ref, v_ref, b_ref, o_ref, *, rows):
    kh, band, n_groups = _na_geometry(rows)
    gq = NA_GROUP * GRID_W
    nk = band * GRID_W
    lane = lax.broadcasted_iota(jnp.int32, (gq, LANE), 1)
    first = lane < HEAD_DIM
    scale = HEAD_DIM ** -0.5

    def body(g, carry):
        r0 = g * NA_GROUP
        bs = jnp.clip(r0 - kh // 2, 0, rows - band)
        q = q_ref[0, pl.ds(pl.multiple_of(r0 * GRID_W, gq), gq), :]
        kb = k_ref[0, pl.ds(pl.multiple_of(bs * GRID_W, GRID_W), nk), :]
        vb = v_ref[0, pl.ds(pl.multiple_of(bs * GRID_W, GRID_W), nk), :]
        zero = jnp.zeros_like(q)
        qq = jnp.concatenate([jnp.where(first, q, zero), jnp.where(first, zero, q)], axis=0)
        s = lax.dot_general(qq, kb, (((1,), (1,)), ((), ())), preferred_element_type=F32)
        s = s * (scale * LOG2_E) + b_ref[0, var_ref[g]]
        m = jnp.max(s, axis=-1, keepdims=True)
        p = jnp.exp2(s - m)
        l = jnp.sum(p, axis=-1, keepdims=True)
        o = jnp.dot(p.astype(BF16), vb, preferred_element_type=F32) / l
        out = jnp.where(first, o[:gq], o[gq:])
        o_ref[0, pl.ds(pl.multiple_of(r0 * GRID_W, gq), gq), :] = out.astype(o_ref.dtype)
        return carry

    lax.fori_loop(0, n_groups, body, 0, unroll=NA_GROUP_UNROLL)


def _neighbourhood_attention(proj, bias_tab, variant_of_group, batch, seq):
    rows = seq // GRID_W
    npair = NA_HEADS // 2
    return pl.pallas_call(
        functools.partial(_na_kernel, rows=rows),
        grid_spec=pltpu.PrefetchScalarGridSpec(
            num_scalar_prefetch=1,
            grid=(batch, npair),
            in_specs=[pl.BlockSpec((1, seq, LANE), lambda b, h, var: (b, 0, h)),
                      pl.BlockSpec((1, seq, LANE), lambda b, h, var: (b, 0, npair + h)),
                      pl.BlockSpec((1, seq, LANE), lambda b, h, var: (b, 0, 2 * npair + h)),
                      pl.BlockSpec((1,) + bias_tab.shape[1:], lambda b, h, var: (h, 0, 0, 0))],
            out_specs=pl.BlockSpec((1, seq, LANE), lambda b, h, var: (b, 0, h))),
        out_shape=jax.ShapeDtypeStruct((batch, seq, TOKEN_W), BF16),
        compiler_params=_cparams("parallel", "arbitrary"),
        name="na_attention",
    )(variant_of_group, proj, proj, proj, bias_tab)


def _conv_silu_kernel(x_ref, w_ref, b_ref, s_ref, o_ref, *, seq):
    x = x_ref[0]
    t = lax.broadcasted_iota(jnp.int32, x.shape, 0)
    acc = jnp.zeros_like(x) + b_ref[...]
    for j in range(CONV_K):
        sh = CONV_K // 2 - j
        if sh == 0:
            xs = x
        elif sh > 0:
            xs = jnp.where(t >= sh, pltpu.roll(x, sh, 0), 0.0)
        else:
            xs = jnp.where(t < seq + sh, pltpu.roll(x, seq + sh, 0), 0.0)
        acc = acc + w_ref[j:j + 1, :] * xs
    y = acc * jax.nn.sigmoid(acc) * s_ref[...]
    o_ref[0] = y.astype(o_ref.dtype)


def _conv_silu(proj, conv_w, conv_b, col_scale, batch, seq):
    ncol = 2 * ML_W_PAD // LANE
    kern = functools.partial(_conv_silu_kernel, seq=seq)
    return pl.pallas_call(
        kern,
        grid=(batch, ncol),
        in_specs=[pl.BlockSpec((1, seq, LANE), lambda b, j: (b, 0, j)),
                  pl.BlockSpec((CONV_K, LANE), lambda b, j: (0, j)),
                  pl.BlockSpec((1, LANE), lambda b, j: (0, j)),
                  pl.BlockSpec((1, LANE), lambda b, j: (0, j))],
        out_specs=pl.BlockSpec((1, seq, LANE), lambda b, j: (b, 0, j)),
        out_shape=jax.ShapeDtypeStruct((batch, seq, 2 * ML_W_PAD), BF16),
        compiler_params=_cparams("parallel", "arbitrary"),
        name="ml_conv_silu",
    )(proj, conv_w, conv_b, col_scale)


def _log_sigmoid(x):
    return jnp.minimum(x, 0.0) - jnp.log1p(jnp.exp(-jnp.abs(x)))


def _mlstm_intra_kernel(q_ref, k_ref, v_ref, g_ref, gb_ref,
                        af_ref, ab_ref, kwf_ref, kwb_ref, vx_ref, scf_ref, scb_ref, *, chunks):
    L = ML_CHUNK
    row = lax.broadcasted_iota(jnp.int32, (L, L), 0)
    col = lax.broadcasted_iota(jnp.int32, (L, L), 1)
    hlane = lax.broadcasted_iota(jnp.int32, (L, ML_DH_PAD), 1)
    glane = lax.broadcasted_iota(jnp.int32, (L, LANE), 1)
    is_f = (glane >= 2 * ML_HEADS) & (glane < 4 * ML_HEADS)
    dirs = ((col <= row, L - 1, af_ref, kwf_ref, scf_ref),
            (col >= row, 0, ab_ref, kwb_ref, scb_ref))

    for c in range(chunks):
        rows = slice(c * L, (c + 1) * L)
        g = g_ref[0, rows, :] + gb_ref[...]
        act = jnp.where(is_f, _log_sigmoid(g), g)
        act_t = act.T
        cums = []
        for valid, _, _, _, _ in dirs:
            bcum = jnp.dot(valid.astype(F32), act, preferred_element_type=F32,
                           precision=lax.Precision.HIGHEST)
            cums.append((bcum, bcum.T))
        scs = [jnp.zeros((L, LANE), F32), jnp.zeros((L, LANE), F32)]

        for h in range(ML_HEADS):
            sl = slice(h * ML_DH_PAD, (h + 1) * ML_DH_PAD)
            qh = q_ref[0, rows, sl]
            kh = k_ref[0, rows, sl]
            vh = jnp.where(hlane == ML_DH, 1.0, v_ref[0, rows, sl]).astype(BF16)
            vx_ref[0, rows, sl] = vh
            s_raw = lax.dot_general(qh, kh, (((1,), (1,)), ((), ())), preferred_element_type=F32)
            kf = kh.astype(F32)
            for d, (valid, last, a_ref, kw_ref, _) in enumerate(dirs):
                bcum, bcum_t = cums[d]
                li_lane = d * ML_HEADS + h
                lf_lane = 2 * ML_HEADS + d * ML_HEADS + h
                b_col = bcum[:, lf_lane:lf_lane + 1]
                b_row = bcum_t[lf_lane:lf_lane + 1, :]
                li_col = act[:, li_lane:li_lane + 1]
                li_row = act_t[li_lane:li_lane + 1, :]
                dm = jnp.where(valid, b_col - b_row + li_row, NEG_BIG)
                mloc = jnp.max(dm, axis=-1, keepdims=True)
                p = (s_raw * jnp.exp(dm - mloc)).astype(BF16)
                a_ref[0, rows, sl] = jnp.dot(p, vh, preferred_element_type=F32)
                b_last = b_col[last:last + 1, :]
                cloc = mloc[last:last + 1, :]
                w_src = jnp.exp(b_last - b_col + li_col - cloc)
                kw_ref[0, rows, sl] = (kf * w_src).astype(BF16)
                scs[d] = jnp.where(glane == h, b_col, jnp.where(glane == ML_HEADS + h, mloc, scs[d]))
        scf_ref[0, rows, :] = scs[0]
        scb_ref[0, rows, :] = scs[1]


def _mlstm_inter_kernel(qf_ref, af_ref, kwf_ref, vxf_ref, scf_ref, qb_ref, ab_ref, kwb_ref, vxb_ref, scb_ref,
                        of_ref, ob_ref, c_ref, m_ref):
    L = ML_CHUNK

    @pl.when(pl.program_id(1) == 0)
    def _():
        c_ref[...] = jnp.zeros_like(c_ref)
        m_ref[...] = jnp.zeros_like(m_ref)

    hlane = lax.broadcasted_iota(jnp.int32, (L, ML_DH_PAD), 1)
    dirs = ((L - 1, qf_ref, af_ref, kwf_ref, vxf_ref, scf_ref, of_ref),
            (0, qb_ref, ab_ref, kwb_ref, vxb_ref, scb_ref, ob_ref))
    for d, (last, q_ref, a_ref, kw_ref, vx_ref, sc_ref, o_ref) in enumerate(dirs):
        sc = sc_ref[0]
        for h in range(ML_HEADS):
            sl = slice(h * ML_DH_PAD, (h + 1) * ML_DH_PAD)
            b_col = sc[:, h:h + 1]
            mloc = sc[:, ML_HEADS + h:ML_HEADS + h + 1]
            m_prev = m_ref[d, h][0:1, 0:1]
            inter = b_col + m_prev
            m_t = jnp.maximum(inter, mloc)
            c_old = c_ref[d, h]
            num = (jnp.exp(inter - m_t) * jnp.dot(q_ref[0, :, sl], c_old.astype(BF16), preferred_element_type=F32)
                   + jnp.exp(mloc - m_t) * a_ref[0, :, sl])
            den = num[:, ML_DH:ML_DH + 1]
            hout = num / jnp.maximum(jnp.abs(den), jnp.exp(-m_t))
            o_ref[0, :, sl] = jnp.where(hlane < ML_DH, hout, 0.0)

            b_last = b_col[last:last + 1, :]
            cloc = mloc[last:last + 1, :]
            m_new = m_t[last:last + 1, :]
            upd = lax.dot_general(kw_ref[0, :, sl], vx_ref[0, :, sl], (((0,), (0,)), ((), ())),
                                  preferred_element_type=F32)
            c_ref[d, h] = jnp.exp(b_last + m_prev - m_new) * c_old + jnp.exp(cloc - m_new) * upd
            m_ref[d, h] = jnp.zeros(m_ref.shape[2:], F32) + m_new


def _mlstm(qk, proj, gate_b, batch, seq):
    nc = seq // ML_CHUNK
    chunks = ML_INTRA_CHUNKS
    assert nc % chunks == 0
    tb = chunks * ML_CHUNK
    gcol = (4 * ML_W_PAD + MEM_W) // LANE
    at = lambda col: (lambda b, c: (b, c, col))
    f32w = jax.ShapeDtypeStruct((batch, seq, ML_W_PAD), F32)
    bf16w = jax.ShapeDtypeStruct((batch, seq, ML_W_PAD), BF16)
    scal = jax.ShapeDtypeStruct((batch, seq, LANE), F32)
    wide_p = pl.BlockSpec((1, tb, ML_W_PAD), at(0))
    lane_p = pl.BlockSpec((1, tb, LANE), at(0))
    a_f, a_b, kw_f, kw_b, vx, sc_f, sc_b = pl.pallas_call(
        functools.partial(_mlstm_intra_kernel, chunks=chunks),
        grid=(batch, nc // chunks),
        in_specs=[pl.BlockSpec((1, tb, ML_W_PAD), at(0)),
                  pl.BlockSpec((1, tb, ML_W_PAD), at(1)),
                  pl.BlockSpec((1, tb, ML_W_PAD), at(2)),
                  pl.BlockSpec((1, tb, LANE), at(gcol)),
                  pl.BlockSpec((1, LANE), lambda b, c: (0, 0))],
        out_specs=[wide_p, wide_p, wide_p, wide_p, wide_p, lane_p, lane_p],
        out_shape=[f32w, f32w, bf16w, bf16w, bf16w, scal, scal],
        compiler_params=_cparams("parallel", "parallel"),
        name="mlstm_intra",
    )(qk, qk, proj, proj, gate_b)

    fwd = lambda b, c: (b, c, 0)
    bwd = lambda b, c: (b, nc - 1 - c, 0)
    wide = lambda m: pl.BlockSpec((1, ML_CHUNK, ML_W_PAD), m)
    lane = lambda m: pl.BlockSpec((1, ML_CHUNK, LANE), m)
    return pl.pallas_call(
        _mlstm_inter_kernel,
        grid=(batch, nc),
        in_specs=[wide(fwd), wide(fwd), wide(fwd), wide(fwd), lane(fwd),
                  wide(bwd), wide(bwd), wide(bwd), wide(bwd), lane(bwd)],
        out_specs=[wide(fwd), wide(bwd)],
        out_shape=[f32w, f32w],
        scratch_shapes=[pltpu.VMEM((2, ML_HEADS, ML_DH_PAD, ML_DH_PAD), F32),
                        pltpu.VMEM((2, ML_HEADS, 8, LANE), F32)],
        compiler_params=_cparams("parallel", "arbitrary"),
        name="mlstm_inter",
    )(qk, a_f, kw_f, vx, sc_f, qk, a_b, kw_b, vx, sc_b)


def _layer_norm(y, g, b):
    mu = jnp.mean(y, axis=-1, keepdims=True)
    yc = y - mu
    var = jnp.mean(yc * yc, axis=-1, keepdims=True)
    return yc * lax.rsqrt(var + LN_EPS) * g + b


def _first_argmax(val, lane):
    mx = jnp.max(val, axis=-1, keepdims=True)
    idx = jnp.min(jnp.where(val == mx, lane.astype(F32), float(LANE)), axis=-1, keepdims=True)
    return mx, idx.astype(jnp.int32)


def _route(x1, wr_ref, br_ref):
    x_hi = x1.astype(BF16)
    x_lo = (x1 - x_hi.astype(F32)).astype(BF16)
    t = jnp.dot(x_hi, wr_ref[...], preferred_element_type=F32)
    logits = (t[:, :LANE] + t[:, LANE:] + jnp.dot(x_lo, wr_ref[:, :LANE], preferred_element_type=F32)
              + br_ref[...])
    lane = lax.broadcasted_iota(jnp.int32, logits.shape, 1)
    is_g = lane < N_GROUPS
    gl = jnp.where(is_g, logits, NEG_BIG)
    gmax, gidx = _first_argmax(gl, lane)
    gsum = jnp.sum(jnp.where(is_g, jnp.exp(gl - gmax), 0.0), axis=-1, keepdims=True)
    g_gate = 1.0 / gsum
    lo = N_GROUPS + gidx * EXPERTS_PER_GROUP
    in_grp = (lane >= lo) & (lane < lo + EXPERTS_PER_GROUP)
    el = jnp.where(in_grp, logits, NEG_BIG)
    emax = jnp.max(el, axis=-1, keepdims=True)
    pe = jnp.where(in_grp, jnp.exp(el - emax), 0.0)
    prob = pe / jnp.sum(pe, axis=-1, keepdims=True)
    prob = jnp.where(in_grp, prob, -1.0)
    p1, i1 = _first_argmax(prob, lane)
    p2, i2 = _first_argmax(jnp.where(lane == i1, -1.0, prob), lane)
    tot = p1 + p2
    return i1 - N_GROUPS, i2 - N_GROUPS, g_gate * p1 / tot, g_gate * p2 / tot


def _post_kernel(*refs, ml):
    if ml:
        (hf_ref, hb_ref, op_ref, ng_ref, qm_ref, x_ref, mkv_ref, wo_ref, g_ref, b_ref, wr_ref, br_ref,
         x1_ref, ri_ref, rf_ref, cnt_ref) = refs
        hsum = hf_ref[...] + hb_ref[...]
        hl = lax.broadcasted_iota(jnp.int32, (hsum.shape[0], ML_DH_PAD), 1)
        parts = []
        for h in range(ML_HEADS):
            hh = hsum[:, h * ML_DH_PAD:(h + 1) * ML_DH_PAD]
            mu = jnp.sum(hh, axis=-1, keepdims=True) * (1.0 / ML_DH)
            hc = jnp.where(hl < ML_DH, hh - mu, 0.0)
            var = jnp.sum(hc * hc, axis=-1, keepdims=True) * (1.0 / ML_DH)
            parts.append(hc * lax.rsqrt(var + LN_EPS))
        hn = jnp.concatenate(parts, axis=-1) * ng_ref[...]
        tok = (jax.nn.sigmoid(op_ref[...]) * hn).astype(BF16)
    else:
        (tok_ref, qm_ref, x_ref, mkv_ref, wo_ref, g_ref, b_ref, wr_ref, br_ref,
         x1_ref, ri_ref, rf_ref, cnt_ref) = refs
        tok = tok_ref[...]

    qm = qm_ref[...].astype(BF16)
    mk = mkv_ref[0, :, 0:MEM_W]
    mv = mkv_ref[0, :, MEM_W:2 * MEM_W]
    mlane = lax.broadcasted_iota(jnp.int32, qm.shape, 1)
    matt = jnp.zeros(qm.shape, F32)
    for h in range(MEM_HEADS):
        hm = (mlane >= h * MEM_DH) & (mlane < (h + 1) * MEM_DH)
        qh = jnp.where(hm, qm, jnp.zeros_like(qm))
        s = lax.dot_general(qh, mk, (((1,), (1,)), ((), ())), preferred_element_type=F32) * (MEM_DH ** -0.5)
        p = jnp.exp(s - jnp.max(s, axis=-1, keepdims=True))
        l = jnp.sum(p, axis=-1, keepdims=True)
        o = jnp.dot(p.astype(BF16), mv, preferred_element_type=F32) / l
        matt = jnp.where(hm, o, matt)

    mixed = jnp.concatenate([tok, matt.astype(BF16)], axis=-1)
    y = ALPHA * x_ref[...] + jnp.dot(mixed, wo_ref[...], preferred_element_type=F32)
    x1 = _layer_norm(y, g_ref[...], b_ref[...])
    x1_ref[...] = x1

    e1, e2, g1, g2 = _route(x1, wr_ref, br_ref)
    lane = lax.broadcasted_iota(jnp.int32, ri_ref.shape, 1)
    tm = x1.shape[0]
    oh1 = jnp.where(lane == e1, 1.0, 0.0)
    oh2 = jnp.where(lane == e2, 1.0, 0.0)
    trow = lax.broadcasted_iota(jnp.int32, (tm, tm), 0)
    tcol = lax.broadcasted_iota(jnp.int32, (tm, tm), 1)
    earlier = jnp.where(tcol < trow, 1.0, 0.0).astype(BF16)
    before1 = jnp.dot(earlier, oh1.astype(BF16), preferred_element_type=F32)
    before2 = jnp.dot(earlier, oh2.astype(BF16), preferred_element_type=F32)
    c1 = jnp.sum(oh1, axis=0, keepdims=True)
    c2 = jnp.sum(oh2, axis=0, keepdims=True)
    rank1 = jnp.sum(oh1 * before1, axis=-1, keepdims=True).astype(jnp.int32)
    rank2 = jnp.sum(oh2 * (before2 + c1), axis=-1, keepdims=True).astype(jnp.int32)
    ri_ref[...] = jnp.where(lane == 0, e1, jnp.where(lane == 1, e2,
                            jnp.where(lane == 2, rank1, jnp.where(lane == 3, rank2, 0))))
    rf_ref[...] = jnp.where(lane == 0, g1, jnp.where(lane == 1, g2, 0.0))
    cnt_ref[...] = jnp.broadcast_to(c1 + c2, cnt_ref.shape).astype(jnp.int32)


def _post_mixer(mix_inputs, x, mem_kv, w_out, ln_g, ln_b, w_route, b_route, seq, ml):
    n = x.shape[0]
    tm = POST_TM
    tiles_per_batch = seq // tm
    row = lambda i: (i, 0)
    const = lambda i: (0, 0)
    if ml:
        h_f, h_b, proj, norm_g = mix_inputs
        ins = [h_f, h_b, proj, norm_g, proj]
        specs = [pl.BlockSpec((tm, ML_W_PAD), row),
                 pl.BlockSpec((tm, ML_W_PAD), row),
                 pl.BlockSpec((tm, ML_W_PAD), lambda i: (i, 3)),
                 pl.BlockSpec((1, ML_W_PAD), const),
                 pl.BlockSpec((tm, MEM_W), lambda i: (i, 4 * ML_W_PAD // MEM_W))]
    else:
        tok, proj = mix_inputs
        ins = [tok, proj]
        specs = [pl.BlockSpec((tm, TOKEN_W), row),
                 pl.BlockSpec((tm, MEM_W), lambda i: (i, 3 * TOKEN_W // MEM_W))]
    ins += [x, mem_kv, w_out, ln_g, ln_b, w_route, b_route]
    specs += [pl.BlockSpec((tm, D_MODEL), row),
              pl.BlockSpec((1, N_MEM, 2 * MEM_W), lambda i: (i // tiles_per_batch, 0, 0)),
              pl.BlockSpec(w_out.shape, const),
              pl.BlockSpec((1, D_MODEL), const),
              pl.BlockSpec((1, D_MODEL), const),
              pl.BlockSpec((D_MODEL, 2 * LANE), const),
              pl.BlockSpec((1, LANE), const)]
    return pl.pallas_call(
        functools.partial(_post_kernel, ml=ml),
        grid=(n // tm,),
        in_specs=specs,
        out_specs=[pl.BlockSpec((tm, D_MODEL), row),
                   pl.BlockSpec((tm, LANE), row),
                   pl.BlockSpec((tm, LANE), row),
                   pl.BlockSpec((8, LANE), row)],
        out_shape=[jax.ShapeDtypeStruct((n, D_MODEL), F32),
                   jax.ShapeDtypeStruct((n, LANE), jnp.int32),
                   jax.ShapeDtypeStruct((n, LANE), F32),
                   jax.ShapeDtypeStruct((n // tm * 8, LANE), jnp.int32)],
        compiler_params=_cparams("parallel"),
        name="post_mixer_ml" if ml else "post_mixer_na",
    )(*ins)


def _start_segment_copies(tile, dst_ref, cnt_ref, loff_ref, make_copy):
    def seg(e, carry):
        idx = tile * N_EXPERTS + e
        n8 = cnt_ref[idx]
        l0 = loff_ref[idx]
        g0 = dst_ref[idx]

        def piece(b):
            size = SEG_ALIGN << b
            done = n8 & (size - 1)

            @pl.when((n8 & size) != 0)
            def _():
                make_copy(pl.multiple_of(l0 + done, SEG_ALIGN), pl.multiple_of(g0 + done, SEG_ALIGN), size).start()

        for b in range(SEG_SMALL_BITS):
            piece(b)

        @pl.when(n8 >= (SEG_ALIGN << SEG_SMALL_BITS))
        def _():
            for b in range(SEG_SMALL_BITS, SEG_BITS):
                piece(b)
        return carry

    lax.fori_loop(0, N_EXPERTS, seg, 0)


def _wait_rows(total, make_desc):
    for b in range((LBUF // SEG_ALIGN).bit_length()):
        size = SEG_ALIGN << b

        @pl.when((total & size) != 0)
        def _():
            make_desc(size).wait()


def _pack_bf16_pairs(x, rounded):
    if not rounded:
        x = x.astype(BF16).astype(F32)
    lo = lax.bitcast_convert_type(x[:, :PACKED_W], jnp.uint32) >> 16
    hi = lax.bitcast_convert_type(x[:, PACKED_W:], jnp.uint32) & jnp.uint32(0xFFFF0000)
    return lo | hi


def _unpack_bf16_pairs(u):
    lo = lax.bitcast_convert_type(u << 16, F32)
    hi = lax.bitcast_convert_type(u & jnp.uint32(0xFFFF0000), F32)
    return jnp.concatenate([lo, hi], axis=-1).astype(BF16)


def _local_positions(ri, lrow):
    lane = lax.broadcasted_iota(jnp.int32, ri.shape, 1)
    pos = []
    for k in range(TOP_K):
        off = jnp.sum(jnp.where(lane == ri[:, k:k + 1], lrow, 0).astype(F32), axis=-1, keepdims=True)
        pos.append(off.astype(jnp.int32) + ri[:, TOP_K + k:TOP_K + k + 1])
    return pos


def _dispatch_kernel(dst_ref, cnt_ref, loff_ref, tot_ref, zdst_ref, zcnt_ref, nused_ref,
                     x_ref, ri_ref, rf_ref, lrow_ref, xs_ref, sbuf, zbuf, sems):
    t = pl.program_id(0)
    cur = lax.rem(t, 2)
    tm = x_ref.shape[0]
    p0, p1 = _local_positions(ri_ref[...], lrow_ref[0, 0:1, :])
    lane = lax.broadcasted_iota(jnp.int32, (tm, LANE), 1)
    both = jnp.where(lane == 0, p0, jnp.where(lane == 1, p1, -1)).astype(F32)
    both_t = both.T.astype(jnp.int32)
    row = lax.broadcasted_iota(jnp.int32, (LBUF, tm), 0)
    hits = [row == both_t[k:k + 1, :] for k in range(TOP_K)]
    perm = jnp.where(hits[0] | hits[1], 1.0, 0.0).astype(BF16)
    sbuf[cur, :, 0:PACKED_W] = _pack_bf16_pairs(
        jnp.dot(perm, x_ref[...].astype(BF16), preferred_element_type=F32), rounded=True)
    gates_t = rf_ref[...].T
    gate = jnp.sum(jnp.where(hits[0], gates_t[0:1, :], 0.0) + jnp.where(hits[1], gates_t[1:2, :], 0.0),
                   axis=-1, keepdims=True)
    sbuf[cur, :, PACKED_W:] = lax.bitcast_convert_type(jnp.broadcast_to(gate, (LBUF, LANE)), jnp.uint32)

    def seg_copy(slot):
        def make(lrow, grow, n):
            return pltpu.make_async_copy(sbuf.at[slot, pl.ds(lrow, n)], xs_ref.at[pl.ds(grow, n)], sems.at[slot])
        return make

    _start_segment_copies(t, dst_ref, cnt_ref, loff_ref, seg_copy(cur))

    @pl.when(t > 0)
    def _():
        _wait_rows(tot_ref[t - 1], lambda n: seg_copy(1 - cur)(0, 0, n))

    @pl.when(t == pl.num_programs(0) - 1)
    def _():
        _wait_rows(tot_ref[t], lambda n: seg_copy(cur)(0, 0, n))
        sem = sems.at[0]
        zbuf[...] = jnp.zeros_like(zbuf)
        for wait in (False, True):
            def fill(e, carry):
                n8 = zcnt_ref[e]
                g0 = zdst_ref[e]
                for b in range((MOE_BLK // SEG_ALIGN).bit_length() - 1):
                    size = SEG_ALIGN << b
                    done = n8 & (size - 1)

                    @pl.when((n8 & size) != 0)
                    def _():
                        cp = pltpu.make_async_copy(zbuf.at[pl.ds(0, size)],
                                                   xs_ref.at[pl.ds(pl.multiple_of(g0 + done, SEG_ALIGN), size)], sem)
                        if wait:
                            cp.wait()
                        else:
                            cp.start()
                return carry
            lax.fori_loop(0, N_EXPERTS, fill, 0)

            def fill_tail(h, carry):
                cp = pltpu.make_async_copy(zbuf, xs_ref.at[pl.ds(pl.multiple_of(h * zbuf.shape[0], SEG_ALIGN),
                                                                 zbuf.shape[0])], sem)
                if wait:
                    cp.wait()
                else:
                    cp.start()
                return carry
            halves = MOE_BLK // zbuf.shape[0]
            lax.fori_loop(nused_ref[0] * halves, xs_ref.shape[0] // zbuf.shape[0], fill_tail, 0)


def _dispatch(plan, x1, route_i, route_f, n_rows):
    n = x1.shape[0]
    tm = POST_TM
    row = lambda i, *_: (i, 0)
    return pl.pallas_call(
        _dispatch_kernel,
        grid_spec=pltpu.PrefetchScalarGridSpec(
            num_scalar_prefetch=7,
            grid=(n // tm,),
            in_specs=[pl.BlockSpec((tm, D_MODEL), row),
                      pl.BlockSpec((tm, LANE), row),
                      pl.BlockSpec((tm, LANE), row),
                      pl.BlockSpec((1, 8, LANE), lambda i, *_: (i, 0, 0)),
                      ],
            out_specs=pl.BlockSpec(memory_space=pl.ANY),
            scratch_shapes=[pltpu.VMEM((2, LBUF, XS_W), jnp.uint32),
                            pltpu.VMEM((MOE_BLK // 2, XS_W), jnp.uint32),
                            pltpu.SemaphoreType.DMA((2,))]),
        out_shape=jax.ShapeDtypeStruct((n_rows, XS_W), jnp.uint32),
        compiler_params=_cparams("arbitrary"),
        name="moe_dispatch",
    )(plan["seg_dst"], plan["seg_cnt"], plan["seg_loff"], plan["seg_tot"], plan["fill_dst"], plan["fill_cnt"],
      plan["n_used"], x1, route_i, route_f, plan["loff_rows"])


def _expert_kernel(beid_ref, nused_ref, x_ref, w1_ref, w3_ref, w2_ref, o_ref, w1b, w3b, w2b):
    i = pl.program_id(0)
    prev = beid_ref[jnp.maximum(i - 1, 0)]
    fresh = (i == 0) | (beid_ref[i] != prev)

    @pl.when(fresh)
    def _():
        w1b[...] = w1_ref[...].astype(BF16)
        w3b[...] = w3_ref[...].astype(BF16)
        w2b[...] = w2_ref[...].astype(BF16)

    @pl.when(i < nused_ref[0])
    def _():
        x = _unpack_bf16_pairs(x_ref[:, 0:PACKED_W])
        gate = lax.bitcast_convert_type(x_ref[:, PACKED_W:PACKED_W + 1], F32)
        a = jnp.dot(x, w1b[...], preferred_element_type=F32)
        b = jnp.dot(x, w3b[...], preferred_element_type=F32)
        hmid = (a * jax.nn.sigmoid(a) * b).astype(BF16)
        o_ref[...] = _pack_bf16_pairs(jnp.dot(hmid, w2b[...], preferred_element_type=F32) * gate, rounded=False)

    @pl.when(i >= nused_ref[0])
    def _():
        o_ref[...] = jnp.zeros_like(o_ref)


def _expert_mlp(block_eid, n_used, xs, w1, w3, w2, layer):
    p = xs.shape[0]
    nb = p // MOE_BLK
    wmap13 = lambda i, be, nu: (layer, be[i], 0, 0)
    return pl.pallas_call(
        _expert_kernel,
        grid_spec=pltpu.PrefetchScalarGridSpec(
            num_scalar_prefetch=2,
            grid=(nb,),
            in_specs=[pl.BlockSpec((MOE_BLK, XS_W), lambda i, be, nu: (jnp.maximum(jnp.minimum(i, nu[0] - 1), 0), 0)),
                      pl.BlockSpec((None, None, D_MODEL, D_EXPERT), wmap13),
                      pl.BlockSpec((None, None, D_MODEL, D_EXPERT), wmap13),
                      pl.BlockSpec((None, None, D_EXPERT, D_MODEL), wmap13)],
            out_specs=pl.BlockSpec((MOE_BLK, PACKED_W), lambda i, be, nu: (i, 0)),
            scratch_shapes=[pltpu.VMEM((D_MODEL, D_EXPERT), BF16),
                            pltpu.VMEM((D_MODEL, D_EXPERT), BF16),
                            pltpu.VMEM((D_EXPERT, D_MODEL), BF16)]),
        out_shape=jax.ShapeDtypeStruct((p, PACKED_W), jnp.uint32),
        compiler_params=_cparams("arbitrary"),
        name="moe_experts",
    )(block_eid, n_used, xs, w1, w3, w2)


def _combine_kernel(dst_ref, cnt_ref, loff_ref, tot_ref, x_ref, ri_ref, lrow_ref, g_ref, b_ref, ys_ref,
                    o_ref, ybuf, sems):
    t = pl.program_id(0)
    cur = lax.rem(t, 2)

    def seg_copy(slot):
        def make(lrow, grow, n):
            return pltpu.make_async_copy(ys_ref.at[pl.ds(grow, n)], ybuf.at[slot, pl.ds(lrow, n)], sems.at[slot])
        return make

    @pl.when(t == 0)
    def _():
        ybuf[...] = jnp.zeros_like(ybuf)
        _start_segment_copies(t, dst_ref, cnt_ref, loff_ref, seg_copy(cur))

    @pl.when(t + 1 < pl.num_programs(0))
    def _():
        _start_segment_copies(t + 1, dst_ref, cnt_ref, loff_ref, seg_copy(1 - cur))

    p0, p1 = _local_positions(ri_ref[...], lrow_ref[0, 0:1, :])
    col = lax.broadcasted_iota(jnp.int32, (x_ref.shape[0], LBUF), 1)
    sel = jnp.where((col == p0) | (col == p1), 1.0, 0.0).astype(BF16)
    _wait_rows(tot_ref[t], lambda n: seg_copy(cur)(0, 0, n))

    ffn = jnp.dot(sel, _unpack_bf16_pairs(ybuf[cur]), preferred_element_type=F32)
    o_ref[...] = _layer_norm(ALPHA * x_ref[...] + ffn, g_ref[...], b_ref[...])


def _combine_norm(plan, x1, route_i, ys, ln_g, ln_b):
    n = x1.shape[0]
    tm = POST_TM
    row = lambda i, *_: (i, 0)
    const = lambda i, *_: (0, 0)
    return pl.pallas_call(
        _combine_kernel,
        grid_spec=pltpu.PrefetchScalarGridSpec(
            num_scalar_prefetch=4,
            grid=(n // tm,),
            in_specs=[pl.BlockSpec((tm, D_MODEL), row),
                      pl.BlockSpec((tm, LANE), row),
                      pl.BlockSpec((1, 8, LANE), lambda i, *_: (i, 0, 0)),
                      pl.BlockSpec((1, D_MODEL), const),
                      pl.BlockSpec((1, D_MODEL), const),
                      pl.BlockSpec(memory_space=pl.ANY)],
            out_specs=pl.BlockSpec((tm, D_MODEL), row),
            scratch_shapes=[pltpu.VMEM((2, LBUF, PACKED_W), jnp.uint32),
                            pltpu.SemaphoreType.DMA((2,))]),
        out_shape=jax.ShapeDtypeStruct((n, D_MODEL), F32),
        compiler_params=_cparams("arbitrary"),
        name="moe_combine_norm",
    )(plan["seg_dst"], plan["seg_cnt"], plan["seg_loff"], plan["seg_tot"], x1, route_i,
      plan["loff_rows"], ln_g, ln_b, ys)


def _routing_plan(tile_counts, n_tokens):
    n_tiles = tile_counts.shape[0]
    e_before = jnp.arange(N_EXPERTS)[:, None] < jnp.arange(N_EXPERTS)[None, :]
    t_before = jnp.arange(n_tiles)[:, None] < jnp.arange(n_tiles)[None, :]
    cnt8 = (tile_counts + SEG_ALIGN - 1) // SEG_ALIGN * SEG_ALIGN
    loff = jnp.sum(jnp.where(e_before[None], cnt8[:, :, None], 0), axis=1)
    tot8 = jnp.sum(cnt8, axis=0)
    padded = (tot8 + MOE_BLK - 1) // MOE_BLK * MOE_BLK
    pad_start = jnp.sum(jnp.where(e_before, padded[:, None], 0), axis=0)
    pad_end = pad_start + padded
    seg_dst = pad_start[None, :] + jnp.sum(jnp.where(t_before[:, :, None], cnt8[:, None, :], 0), axis=0)
    worst_rows = n_tokens * TOP_K + n_tiles * N_EXPERTS * (SEG_ALIGN - 1) + N_EXPERTS * (MOE_BLK - 1)
    n_blocks = -(-worst_rows // MOE_BLK)
    block_row = jnp.arange(n_blocks, dtype=jnp.int32) * MOE_BLK
    block_eid = jnp.minimum(jnp.sum(pad_end[None, :] <= block_row[:, None], axis=1), N_EXPERTS - 1).astype(jnp.int32)
    i32 = lambda v: v.astype(jnp.int32)
    loff_rows = jnp.broadcast_to(jnp.pad(i32(loff), ((0, 0), (0, LANE - N_EXPERTS)))[:, None, :],
                                 (n_tiles, 8, LANE))
    return dict(seg_dst=i32(seg_dst).reshape(-1), seg_cnt=i32(cnt8).reshape(-1), seg_loff=i32(loff).reshape(-1),
                seg_tot=i32(jnp.sum(cnt8, axis=1)),
                fill_dst=i32(pad_start + tot8), fill_cnt=i32(padded - tot8), loff_rows=loff_rows,
                block_eid=block_eid, n_used=i32(pad_end[-1] // MOE_BLK).reshape(1), n_rows=n_blocks * MOE_BLK)


def _moe(x1, route_i, route_f, tile_cnt, w1, w3, w2, ln_g, ln_b, layer):
    n = x1.shape[0]
    counts = tile_cnt.reshape(n // POST_TM, 8, LANE)[:, 0, :N_EXPERTS]
    plan = _routing_plan(counts, n)
    xs = _dispatch(plan, x1, route_i, route_f, plan["n_rows"])
    ys = _expert_mlp(plan["block_eid"], plan["n_used"], xs, w1, w3, w2, layer)
    return _combine_norm(plan, x1, route_i, ys, ln_g, ln_b)


def _pad_heads(w):
    lead = w.shape[:-1]
    w = w.reshape(*lead, ML_HEADS, ML_DH)
    w = jnp.pad(w, [(0, 0)] * len(lead) + [(0, 0), (0, ML_DH_PAD - ML_DH)])
    return w.reshape(*lead, ML_W_PAD)


def _ml_in_weight(w):
    tw = TOKEN_W
    segs = [_pad_heads(w[:, i * tw:(i + 1) * tw]) for i in range(4)]
    gates = w[:, 4 * tw:4 * tw + 4 * ML_HEADS]
    qmem = w[:, 4 * tw + 4 * ML_HEADS:]
    gates = jnp.pad(gates, ((0, 0), (0, LANE - 4 * ML_HEADS)))
    return jnp.concatenate(segs + [qmem, gates], axis=1)


def _route_weight(wg, bg, we, be):
    w = jnp.concatenate([wg, we], axis=1)
    b = jnp.concatenate([bg, be], axis=0)
    pad = LANE - w.shape[1]
    w = jnp.pad(w, ((0, 0), (0, pad))).astype(F32)
    w_hi = w.astype(BF16)
    w_lo = (w - w_hi.astype(F32)).astype(BF16)
    return jnp.concatenate([w_hi, w_lo], axis=1), jnp.pad(b, (0, pad)).reshape(1, LANE).astype(F32)


def kernel(x, mem, w_mem_kv, na_w_in, na_rpb, ml_w_in, ml_conv_w, ml_conv_b, ml_gate_b, ml_norm_g, w_out,
           ln1_g, ln1_b, ln2_g, ln2_b, moe_wg, moe_bg, moe_we, moe_be, moe_w1, moe_w3, moe_w2):
    batch, seq, d = x.shape
    n = batch * seq
    rows = seq // GRID_W
    xf = x.reshape(n, d)

    mem_kv = _matmul(mem.reshape(batch * N_MEM, d), w_mem_kv.astype(BF16), BF16,
                     min(PROJ_TM, batch * N_MEM), 2 * MEM_W, "mem_kv").reshape(batch, N_MEM, 2 * MEM_W)

    for layer in range(DEPTH):
        j = layer // 2
        w_route, b_route = _route_weight(moe_wg[layer], moe_bg[layer], moe_we[layer], moe_be[layer])
        g1 = ln1_g[layer].reshape(1, d)
        b1 = ln1_b[layer].reshape(1, d)
        if layer % 2 == 0:
            proj = _matmul(xf, na_w_in[j].astype(BF16), BF16, PROJ_TM, PROJ_TN, "na_in_proj")
            bias_tab, variant_of_group = _na_bias_table(na_rpb[j], rows)
            tok = _neighbourhood_attention(proj.reshape(batch, seq, -1), bias_tab, variant_of_group, batch, seq)
            x1, route_i, route_f, tile_cnt = _post_mixer((tok.reshape(n, TOKEN_W), proj), xf, mem_kv,
                                               w_out[layer].astype(BF16), g1, b1, w_route, b_route, seq, ml=False)
        else:
            proj = _matmul(xf, _ml_in_weight(ml_w_in[j]).astype(BF16), F32, PROJ_TM, PROJ_TN, "ml_in_proj")
            proj3 = proj.reshape(batch, seq, -1)
            conv_w = jnp.concatenate([_pad_heads(ml_conv_w[j][:, :TOKEN_W]),
                                      _pad_heads(ml_conv_w[j][:, TOKEN_W:])], axis=1)
            conv_b = jnp.concatenate([_pad_heads(ml_conv_b[j][:TOKEN_W]),
                                      _pad_heads(ml_conv_b[j][TOKEN_W:])]).reshape(1, 2 * ML_W_PAD)
            col_scale = jnp.concatenate([jnp.full((ML_W_PAD,), ML_DH ** -0.5, F32),
                                         jnp.ones((ML_W_PAD,), F32)]).reshape(1, 2 * ML_W_PAD)
            qk = _conv_silu(proj3, conv_w, conv_b, col_scale, batch, seq)
            gate_b = jnp.pad(ml_gate_b[j].reshape(-1), (0, LANE - 4 * ML_HEADS)).reshape(1, LANE)
            h_f, h_b = _mlstm(qk, proj3, gate_b, batch, seq)
            norm_g = _pad_heads(ml_norm_g[j]).reshape(1, ML_W_PAD)
            wo = jnp.concatenate([_pad_heads(w_out[layer][:TOKEN_W].T).T, w_out[layer][TOKEN_W:]], axis=0)
            x1, route_i, route_f, tile_cnt = _post_mixer((h_f.reshape(n, ML_W_PAD), h_b.reshape(n, ML_W_PAD), proj, norm_g),
                                               xf, mem_kv, wo.astype(BF16), g1, b1, w_route, b_route, seq, ml=True)
        xf = _moe(x1, route_i, route_f, tile_cnt, moe_w1, moe_w3, moe_w2,
                  ln2_g[layer].reshape(1, d), ln2_b[layer].reshape(1, d), layer)
    return xf.reshape(batch, seq, d)
```

```python
import functools
import math

import jax
import jax.numpy as jnp
from jax import lax
from jax.experimental import pallas as pl
from jax.experimental.pallas import tpu as pltpu

D_MODEL = 1024
DEPTH = 4
N_MEM = 256
GRID_W = 64
TOKEN_W = 768
MEM_W = 256
HEAD_DIM = 64
NA_HEADS = 12
WIN_H = 8
WIN_W = 16
ML_HEADS = 4
ML_DH = 192
ML_CHUNK = 256
CONV_K = 5
MEM_HEADS = 4
MEM_DH = 64
N_GROUPS = 4
EXPERTS_PER_GROUP = 8
N_EXPERTS = 32
TOP_K = 2
D_EXPERT = 512
LN_EPS = 1e-5
ALPHA = (2 * DEPTH) ** 0.25

LANE = 128
ML_DH_PAD = 256
ML_W_PAD = ML_HEADS * ML_DH_PAD
VMEM_LIMIT_BYTES = 56 * 1024 * 1024

PROJ_TM = 512
PROJ_TN = 640
NA_GROUP = 4
NA_GROUP_UNROLL = 8
POST_TM = 512
MOE_BLK = 512
ML_INTRA_CHUNKS = 1
ML_INTER_SEQS = 1
SEG_ALIGN = 8
SEG_BITS = (TOP_K * POST_TM // SEG_ALIGN).bit_length()
SEG_SMALL_BITS = 4
PACKED_W = D_MODEL // 2
XS_W = PACKED_W + LANE
LBUF = -(-(TOP_K * POST_TM + N_EXPERTS * (SEG_ALIGN - 1)) // 256) * 256
NEG_BIG = -1e30
LOG2_E = math.log2(math.e)

F32 = jnp.float32
BF16 = jnp.bfloat16


def _cparams(*sem):
    return pltpu.CompilerParams(dimension_semantics=sem, vmem_limit_bytes=VMEM_LIMIT_BYTES)


def _matmul_kernel(x_ref, w_ref, o_ref, *, tn):
    x = x_ref[...].astype(BF16)
    for j in range(w_ref.shape[1] // tn):
        cols = slice(j * tn, (j + 1) * tn)
        o_ref[:, cols] = jnp.dot(x, w_ref[:, cols], preferred_element_type=F32).astype(o_ref.dtype)


def _matmul(x, w, out_dtype, tm, tn, name):
    m, k = x.shape
    n = w.shape[1]
    assert m % tm == 0 and n % tn == 0
    return pl.pallas_call(
        functools.partial(_matmul_kernel, tn=tn),
        grid=(m // tm,),
        in_specs=[pl.BlockSpec((tm, k), lambda i: (i, 0)),
                  pl.BlockSpec((k, n), lambda i: (0, 0))],
        out_specs=pl.BlockSpec((tm, n), lambda i: (i, 0)),
        out_shape=jax.ShapeDtypeStruct((m, n), out_dtype),
        compiler_params=_cparams("parallel"),
        name=name,
    )(x, w)


def _na_bias_table(rpb, rows):
    kh, band, n_groups = _na_geometry(rows)
    g_rows = NA_GROUP
    keys, variant_of_group = [], []
    for g in range(n_groups):
        r0 = g * g_rows
        bs = min(max(r0 - kh // 2, 0), rows - band)
        key = tuple((r0 + i - bs, min(max(r0 + i - kh // 2, 0), rows - kh) - bs) for i in range(g_rows))
        if key not in keys:
            keys.append(key)
        variant_of_group.append(keys.index(key))
    geo = jnp.asarray(keys, jnp.int32)
    qrow, wstart = geo[:, :, 0:1], geo[:, :, 1:2]
    j = jnp.arange(band)[None, None, :]
    row_ok = (j >= wstart) & (j < wstart + kh)
    ri = j - qrow + (WIN_H - 1)
    c = jnp.arange(GRID_W)[:, None]
    kc = jnp.arange(GRID_W)[None, :]
    cs = jnp.clip(c - WIN_W // 2, 0, GRID_W - WIN_W)
    col_ok = (kc >= cs) & (kc < cs + WIN_W)
    ci = kc - c + (WIN_W - 1)
    row_sel = ((ri[..., None] == jnp.arange(2 * WIN_H - 1)) & row_ok[..., None]).astype(F32)
    col_sel = ((ci[:, :, None] == jnp.arange(2 * WIN_W - 1)) & col_ok[..., None]).astype(F32)
    pairs = rpb.astype(F32).reshape(NA_HEADS // 2, 2, 2 * WIN_H - 1, 2 * WIN_W - 1)
    tab = jnp.einsum('phab,vija,ckb->pvhicjk', pairs, row_sel, col_sel, precision=lax.Precision.HIGHEST)
    ok = row_ok[:, :, None, :, None] & col_ok[None, None, :, None, :]
    tab = jnp.where(ok[None, :, None], tab * LOG2_E, NEG_BIG)
    tab = tab.reshape(NA_HEADS // 2, len(keys), 2 * g_rows * GRID_W, band * GRID_W)
    return tab.astype(F32), jnp.asarray(variant_of_group, jnp.int32)


def _na_geometry(rows):
    kh = min(WIN_H, rows)
    band = kh + NA_GROUP - 1
    band = min(band + band % 2, rows)
    assert rows % NA_GROUP == 0
    return kh, band, rows // NA_GROUP


def _na_kernel(var_ref, q_ref, k_ref, v_ref, b_ref, o_ref, *, rows):
    kh, band, n_groups = _na_geometry(rows)
    gq = NA_GROUP * GRID_W
    nk = band * GRID_W
    lane = lax.broadcasted_iota(jnp.int32, (gq, LANE), 1)
    first = lane < HEAD_DIM
    scale = HEAD_DIM ** -0.5

    def body(g, carry):
        r0 = g * NA_GROUP
        bs = jnp.clip(r0 - kh // 2, 0, rows - band)
        q = q_ref[0, pl.ds(pl.multiple_of(r0 * GRID_W, gq), gq), :]
        kb = k_ref[0, pl.ds(pl.multiple_of(bs * GRID_W, GRID_W), nk), :]
        vb = v_ref[0, pl.ds(pl.multiple_of(bs * GRID_W, GRID_W), nk), :]
        zero = jnp.zeros_like(q)
        qq = jnp.concatenate([jnp.where(first, q, zero), jnp.where(first, zero, q)], axis=0)
        s = lax.dot_general(qq, kb, (((1,), (1,)), ((), ())), preferred_element_type=F32)
        s = s * (scale * LOG2_E) + b_ref[0, var_ref[g]]
        m = jnp.max(s, axis=-1, keepdims=True)
        p = jnp.exp2(s - m)
        l = jnp.sum(p, axis=-1, keepdims=True)
        o = jnp.dot(p.astype(BF16), vb, preferred_element_type=F32) / l
        out = jnp.where(first, o[:gq], o[gq:])
        o_ref[0, pl.ds(pl.multiple_of(r0 * GRID_W, gq), gq), :] = out.astype(o_ref.dtype)
        return carry

    lax.fori_loop(0, n_groups, body, 0, unroll=NA_GROUP_UNROLL)


def _neighbourhood_attention(proj, bias_tab, variant_of_group, batch, seq):
    rows = seq // GRID_W
    npair = NA_HEADS // 2
    return pl.pallas_call(
        functools.partial(_na_kernel, rows=rows),
        grid_spec=pltpu.PrefetchScalarGridSpec(
            num_scalar_prefetch=1,
            grid=(batch, npair),
            in_specs=[pl.BlockSpec((1, seq, LANE), lambda b, h, var: (b, 0, h)),
                      pl.BlockSpec((1, seq, LANE), lambda b, h, var: (b, 0, npair + h)),
                      pl.BlockSpec((1, seq, LANE), lambda b, h, var: (b, 0, 2 * npair + h)),
                      pl.BlockSpec((1,) + bias_tab.shape[1:], lambda b, h, var: (h, 0, 0, 0))],
            out_specs=pl.BlockSpec((1, seq, LANE), lambda b, h, var: (b, 0, h))),
        out_shape=jax.ShapeDtypeStruct((batch, seq, TOKEN_W), BF16),
        compiler_params=_cparams("parallel", "arbitrary"),
        name="na_attention",
    )(variant_of_group, proj, proj, proj, bias_tab)


def _conv_silu_kernel(x_ref, w_ref, b_ref, s_ref, o_ref, *, seq):
    x = x_ref[0]
    t = lax.broadcasted_iota(jnp.int32, x.shape, 0)
    acc = jnp.zeros_like(x) + b_ref[...]
    for j in range(CONV_K):
        sh = CONV_K // 2 - j
        if sh == 0:
            xs = x
        elif sh > 0:
            xs = jnp.where(t >= sh, pltpu.roll(x, sh, 0), 0.0)
        else:
            xs = jnp.where(t < seq + sh, pltpu.roll(x, seq + sh, 0), 0.0)
        acc = acc + w_ref[j:j + 1, :] * xs
    y = acc * jax.nn.sigmoid(acc) * s_ref[...]
    o_ref[0] = y.astype(o_ref.dtype)


def _conv_silu(proj, conv_w, conv_b, col_scale, batch, seq):
    ncol = 2 * ML_W_PAD // LANE
    kern = functools.partial(_conv_silu_kernel, seq=seq)
    return pl.pallas_call(
        kern,
        grid=(batch, ncol),
        in_specs=[pl.BlockSpec((1, seq, LANE), lambda b, j: (b, 0, j)),
                  pl.BlockSpec((CONV_K, LANE), lambda b, j: (0, j)),
                  pl.BlockSpec((1, LANE), lambda b, j: (0, j)),
                  pl.BlockSpec((1, LANE), lambda b, j: (0, j))],
        out_specs=pl.BlockSpec((1, seq, LANE), lambda b, j: (b, 0, j)),
        out_shape=jax.ShapeDtypeStruct((batch, seq, 2 * ML_W_PAD), BF16),
        compiler_params=_cparams("parallel", "arbitrary"),
        name="ml_conv_silu",
    )(proj, conv_w, conv_b, col_scale)


def _log_sigmoid(x):
    return jnp.minimum(x, 0.0) - jnp.log1p(jnp.exp(-jnp.abs(x)))


def _mlstm_intra_kernel(q_ref, k_ref, v_ref, g_ref, gb_ref,
                        af_ref, ab_ref, kwf_ref, kwb_ref, vx_ref, scf_ref, scb_ref, *, chunks):
    L = ML_CHUNK
    row = lax.broadcasted_iota(jnp.int32, (L, L), 0)
    col = lax.broadcasted_iota(jnp.int32, (L, L), 1)
    hlane = lax.broadcasted_iota(jnp.int32, (L, ML_DH_PAD), 1)
    glane = lax.broadcasted_iota(jnp.int32, (L, LANE), 1)
    is_f = (glane >= 2 * ML_HEADS) & (glane < 4 * ML_HEADS)
    dirs = ((col <= row, L - 1, af_ref, kwf_ref, scf_ref),
            (col >= row, 0, ab_ref, kwb_ref, scb_ref))

    for c in range(chunks):
        rows = slice(c * L, (c + 1) * L)
        g = g_ref[0, rows, :] + gb_ref[...]
        act = jnp.where(is_f, _log_sigmoid(g), g)
        act_t = act.T
        cums = []
        for valid, _, _, _, _ in dirs:
            bcum = jnp.dot(valid.astype(F32), act, preferred_element_type=F32,
                           precision=lax.Precision.HIGHEST)
            cums.append((bcum, bcum.T))
        scs = [jnp.zeros((L, LANE), F32), jnp.zeros((L, LANE), F32)]

        for h in range(ML_HEADS):
            sl = slice(h * ML_DH_PAD, (h + 1) * ML_DH_PAD)
            qh = q_ref[0, rows, sl]
            kh = k_ref[0, rows, sl]
            vh = jnp.where(hlane == ML_DH, 1.0, v_ref[0, rows, sl]).astype(BF16)
            vx_ref[0, rows, sl] = vh
            s_raw = lax.dot_general(qh, kh, (((1,), (1,)), ((), ())), preferred_element_type=F32)
            kf = kh.astype(F32)
            for d, (valid, last, a_ref, kw_ref, _) in enumerate(dirs):
                bcum, bcum_t = cums[d]
                li_lane = d * ML_HEADS + h
                lf_lane = 2 * ML_HEADS + d * ML_HEADS + h
                b_col = bcum[:, lf_lane:lf_lane + 1]
                b_row = bcum_t[lf_lane:lf_lane + 1, :]
                li_col = act[:, li_lane:li_lane + 1]
                li_row = act_t[li_lane:li_lane + 1, :]
                dm = jnp.where(valid, b_col - b_row + li_row, NEG_BIG)
                mloc = jnp.max(dm, axis=-1, keepdims=True)
                p = (s_raw * jnp.exp(dm - mloc)).astype(BF16)
                a_ref[0, rows, sl] = jnp.dot(p, vh, preferred_element_type=F32)
                b_last = b_col[last:last + 1, :]
                cloc = mloc[last:last + 1, :]
                w_src = jnp.exp(b_last - b_col + li_col - cloc)
                kw_ref[0, rows, sl] = (kf * w_src).astype(BF16)
                scs[d] = jnp.where(glane == h, b_col, jnp.where(glane == ML_HEADS + h, mloc, scs[d]))
        scf_ref[0, rows, :] = scs[0]
        scb_ref[0, rows, :] = scs[1]


def _mlstm_inter_kernel(qf_ref, af_ref, kwf_ref, vxf_ref, scf_ref, qb_ref, ab_ref, kwb_ref, vxb_ref, scb_ref,
                        of_ref, ob_ref, c_ref, m_ref):
    L = ML_CHUNK

    @pl.when(pl.program_id(1) == 0)
    def _():
        c_ref[...] = jnp.zeros_like(c_ref)
        m_ref[...] = jnp.zeros_like(m_ref)

    hlane = lax.broadcasted_iota(jnp.int32, (L, ML_DH_PAD), 1)
    dirs = ((L - 1, qf_ref, af_ref, kwf_ref, vxf_ref, scf_ref, of_ref),
            (0, qb_ref, ab_ref, kwb_ref, vxb_ref, scb_ref, ob_ref))
    for s in range(qf_ref.shape[0]):
        for d, (last, q_ref, a_ref, kw_ref, vx_ref, sc_ref, o_ref) in enumerate(dirs):
            sc = sc_ref[s]
            for h in range(ML_HEADS):
                sl = slice(h * ML_DH_PAD, (h + 1) * ML_DH_PAD)
                b_col = sc[:, h:h + 1]
                mloc = sc[:, ML_HEADS + h:ML_HEADS + h + 1]
                m_prev = m_ref[s, d, h][0:1, 0:1]
                inter = b_col + m_prev
                m_t = jnp.maximum(inter, mloc)
                c_old = c_ref[s, d, h]
                num = (jnp.exp(inter - m_t) * jnp.dot(q_ref[s, :, sl], c_old.astype(BF16),
                                                      preferred_element_type=F32)
                       + jnp.exp(mloc - m_t) * a_ref[s, :, sl])
                den = num[:, ML_DH:ML_DH + 1]
                hout = num / jnp.maximum(jnp.abs(den), jnp.exp(-m_t))
                o_ref[s, :, sl] = jnp.where(hlane < ML_DH, hout, 0.0)

                b_last = b_col[last:last + 1, :]
                cloc = mloc[last:last + 1, :]
                m_new = m_t[last:last + 1, :]
                upd = lax.dot_general(kw_ref[s, :, sl], vx_ref[s, :, sl], (((0,), (0,)), ((), ())),
                                      preferred_element_type=F32)
                c_ref[s, d, h] = jnp.exp(b_last + m_prev - m_new) * c_old + jnp.exp(cloc - m_new) * upd
                m_ref[s, d, h] = jnp.zeros(m_ref.shape[3:], F32) + m_new


def _mlstm(qk, proj, gate_b, batch, seq):
    nc = seq // ML_CHUNK
    chunks = ML_INTRA_CHUNKS
    assert nc % chunks == 0
    tb = chunks * ML_CHUNK
    gcol = (4 * ML_W_PAD + MEM_W) // LANE
    at = lambda col: (lambda b, c: (b, c, col))
    f32w = jax.ShapeDtypeStruct((batch, seq, ML_W_PAD), F32)
    bf16w = jax.ShapeDtypeStruct((batch, seq, ML_W_PAD), BF16)
    scal = jax.ShapeDtypeStruct((batch, seq, LANE), F32)
    wide_p = pl.BlockSpec((1, tb, ML_W_PAD), at(0))
    lane_p = pl.BlockSpec((1, tb, LANE), at(0))
    a_f, a_b, kw_f, kw_b, vx, sc_f, sc_b = pl.pallas_call(
        functools.partial(_mlstm_intra_kernel, chunks=chunks),
        grid=(batch, nc // chunks),
        in_specs=[pl.BlockSpec((1, tb, ML_W_PAD), at(0)),
                  pl.BlockSpec((1, tb, ML_W_PAD), at(1)),
                  pl.BlockSpec((1, tb, ML_W_PAD), at(2)),
                  pl.BlockSpec((1, tb, LANE), at(gcol)),
                  pl.BlockSpec((1, LANE), lambda b, c: (0, 0))],
        out_specs=[wide_p, wide_p, wide_p, wide_p, wide_p, lane_p, lane_p],
        out_shape=[f32w, f32w, bf16w, bf16w, bf16w, scal, scal],
        compiler_params=_cparams("parallel", "parallel"),
        name="mlstm_intra",
    )(qk, qk, proj, proj, gate_b)

    fwd = lambda b, c: (b, c, 0)
    bwd = lambda b, c: (b, nc - 1 - c, 0)
    nseq = ML_INTER_SEQS if batch % ML_INTER_SEQS == 0 else 1
    wide = lambda m: pl.BlockSpec((nseq, ML_CHUNK, ML_W_PAD), m)
    lane = lambda m: pl.BlockSpec((nseq, ML_CHUNK, LANE), m)
    return pl.pallas_call(
        _mlstm_inter_kernel,
        grid=(batch // nseq, nc),
        in_specs=[wide(fwd), wide(fwd), wide(fwd), wide(fwd), lane(fwd),
                  wide(bwd), wide(bwd), wide(bwd), wide(bwd), lane(bwd)],
        out_specs=[wide(fwd), wide(bwd)],
        out_shape=[f32w, f32w],
        scratch_shapes=[pltpu.VMEM((nseq, 2, ML_HEADS, ML_DH_PAD, ML_DH_PAD), F32),
                        pltpu.VMEM((nseq, 2, ML_HEADS, 8, LANE), F32)],
        compiler_params=_cparams("parallel", "arbitrary"),
        name="mlstm_inter",
    )(qk, a_f, kw_f, vx, sc_f, qk, a_b, kw_b, vx, sc_b)


def _layer_norm(y, g, b):
    mu = jnp.mean(y, axis=-1, keepdims=True)
    yc = y - mu
    var = jnp.mean(yc * yc, axis=-1, keepdims=True)
    return yc * lax.rsqrt(var + LN_EPS) * g + b


def _first_argmax(val, lane):
    mx = jnp.max(val, axis=-1, keepdims=True)
    idx = jnp.min(jnp.where(val == mx, lane.astype(F32), float(LANE)), axis=-1, keepdims=True)
    return mx, idx.astype(jnp.int32)


def _route(x1, wr_ref, br_ref):
    x_hi = x1.astype(BF16)
    x_lo = (x1 - x_hi.astype(F32)).astype(BF16)
    t = jnp.dot(x_hi, wr_ref[...], preferred_element_type=F32)
    logits = (t[:, :LANE] + t[:, LANE:] + jnp.dot(x_lo, wr_ref[:, :LANE], preferred_element_type=F32)
              + br_ref[...])
    lane = lax.broadcasted_iota(jnp.int32, logits.shape, 1)
    is_g = lane < N_GROUPS
    gl = jnp.where(is_g, logits, NEG_BIG)
    gmax, gidx = _first_argmax(gl, lane)
    gsum = jnp.sum(jnp.where(is_g, jnp.exp(gl - gmax), 0.0), axis=-1, keepdims=True)
    g_gate = 1.0 / gsum
    lo = N_GROUPS + gidx * EXPERTS_PER_GROUP
    in_grp = (lane >= lo) & (lane < lo + EXPERTS_PER_GROUP)
    el = jnp.where(in_grp, logits, NEG_BIG)
    emax = jnp.max(el, axis=-1, keepdims=True)
    pe = jnp.where(in_grp, jnp.exp(el - emax), 0.0)
    prob = pe / jnp.sum(pe, axis=-1, keepdims=True)
    prob = jnp.where(in_grp, prob, -1.0)
    p1, i1 = _first_argmax(prob, lane)
    p2, i2 = _first_argmax(jnp.where(lane == i1, -1.0, prob), lane)
    tot = p1 + p2
    return i1 - N_GROUPS, i2 - N_GROUPS, g_gate * p1 / tot, g_gate * p2 / tot


def _post_kernel(*refs, ml):
    if ml:
        (hf_ref, hb_ref, op_ref, ng_ref, qm_ref, x_ref, mkv_ref, wo_ref, g_ref, b_ref, wr_ref, br_ref,
         x1_ref, ri_ref, rf_ref, cnt_ref) = refs
        hsum = hf_ref[...] + hb_ref[...]
        hl = lax.broadcasted_iota(jnp.int32, (hsum.shape[0], ML_DH_PAD), 1)
        parts = []
        for h in range(ML_HEADS):
            hh = hsum[:, h * ML_DH_PAD:(h + 1) * ML_DH_PAD]
            mu = jnp.sum(hh, axis=-1, keepdims=True) * (1.0 / ML_DH)
            hc = jnp.where(hl < ML_DH, hh - mu, 0.0)
            var = jnp.sum(hc * hc, axis=-1, keepdims=True) * (1.0 / ML_DH)
            parts.append(hc * lax.rsqrt(var + LN_EPS))
        hn = jnp.concatenate(parts, axis=-1) * ng_ref[...]
        tok = (jax.nn.sigmoid(op_ref[...]) * hn).astype(BF16)
    else:
        (tok_ref, qm_ref, x_ref, mkv_ref, wo_ref, g_ref, b_ref, wr_ref, br_ref,
         x1_ref, ri_ref, rf_ref, cnt_ref) = refs
        tok = tok_ref[...]

    qm = qm_ref[...].astype(BF16)
    mk = mkv_ref[0, :, 0:MEM_W]
    mv = mkv_ref[0, :, MEM_W:2 * MEM_W]
    mlane = lax.broadcasted_iota(jnp.int32, qm.shape, 1)
    matt = jnp.zeros(qm.shape, F32)
    for h in range(MEM_HEADS):
        hm = (mlane >= h * MEM_DH) & (mlane < (h + 1) * MEM_DH)
        qh = jnp.where(hm, qm, jnp.zeros_like(qm))
        s = lax.dot_general(qh, mk, (((1,), (1,)), ((), ())), preferred_element_type=F32) * (MEM_DH ** -0.5)
        p = jnp.exp(s - jnp.max(s, axis=-1, keepdims=True))
        l = jnp.sum(p, axis=-1, keepdims=True)
        o = jnp.dot(p.astype(BF16), mv, preferred_element_type=F32) / l
        matt = jnp.where(hm, o, matt)

    mixed = jnp.concatenate([tok, matt.astype(BF16)], axis=-1)
    y = ALPHA * x_ref[...] + jnp.dot(mixed, wo_ref[...], preferred_element_type=F32)
    x1 = _layer_norm(y, g_ref[...], b_ref[...])
    x1_ref[...] = x1

    e1, e2, g1, g2 = _route(x1, wr_ref, br_ref)
    lane = lax.broadcasted_iota(jnp.int32, ri_ref.shape, 1)
    tm = x1.shape[0]
    oh1 = jnp.where(lane == e1, 1.0, 0.0)
    oh2 = jnp.where(lane == e2, 1.0, 0.0)
    trow = lax.broadcasted_iota(jnp.int32, (tm, tm), 0)
    tcol = lax.broadcasted_iota(jnp.int32, (tm, tm), 1)
    earlier = jnp.where(tcol < trow, 1.0, 0.0).astype(BF16)
    before1 = jnp.dot(earlier, oh1.astype(BF16), preferred_element_type=F32)
    before2 = jnp.dot(earlier, oh2.astype(BF16), preferred_element_type=F32)
    c1 = jnp.sum(oh1, axis=0, keepdims=True)
    c2 = jnp.sum(oh2, axis=0, keepdims=True)
    rank1 = jnp.sum(oh1 * before1, axis=-1, keepdims=True).astype(jnp.int32)
    rank2 = jnp.sum(oh2 * (before2 + c1), axis=-1, keepdims=True).astype(jnp.int32)
    ri_ref[...] = jnp.where(lane == 0, e1, jnp.where(lane == 1, e2,
                            jnp.where(lane == 2, rank1, jnp.where(lane == 3, rank2, 0))))
    rf_ref[...] = jnp.where(lane == 0, g1, jnp.where(lane == 1, g2, 0.0))
    cnt_ref[...] = jnp.broadcast_to(c1 + c2, cnt_ref.shape).astype(jnp.int32)


def _post_mixer(mix_inputs, x, mem_kv, w_out, ln_g, ln_b, w_route, b_route, seq, ml):
    n = x.shape[0]
    tm = POST_TM
    tiles_per_batch = seq // tm
    row = lambda i: (i, 0)
    const = lambda i: (0, 0)
    if ml:
        h_f, h_b, proj, norm_g = mix_inputs
        ins = [h_f, h_b, proj, norm_g, proj]
        specs = [pl.BlockSpec((tm, ML_W_PAD), row),
                 pl.BlockSpec((tm, ML_W_PAD), row),
                 pl.BlockSpec((tm, ML_W_PAD), lambda i: (i, 3)),
                 pl.BlockSpec((1, ML_W_PAD), const),
                 pl.BlockSpec((tm, MEM_W), lambda i: (i, 4 * ML_W_PAD // MEM_W))]
    else:
        tok, proj = mix_inputs
        ins = [tok, proj]
        specs = [pl.BlockSpec((tm, TOKEN_W), row),
                 pl.BlockSpec((tm, MEM_W), lambda i: (i, 3 * TOKEN_W // MEM_W))]
    ins += [x, mem_kv, w_out, ln_g, ln_b, w_route, b_route]
    specs += [pl.BlockSpec((tm, D_MODEL), row),
              pl.BlockSpec((1, N_MEM, 2 * MEM_W), lambda i: (i // tiles_per_batch, 0, 0)),
              pl.BlockSpec(w_out.shape, const),
              pl.BlockSpec((1, D_MODEL), const),
              pl.BlockSpec((1, D_MODEL), const),
              pl.BlockSpec((D_MODEL, 2 * LANE), const),
              pl.BlockSpec((1, LANE), const)]
    return pl.pallas_call(
        functools.partial(_post_kernel, ml=ml),
        grid=(n // tm,),
        in_specs=specs,
        out_specs=[pl.BlockSpec((tm, D_MODEL), row),
                   pl.BlockSpec((tm, LANE), row),
                   pl.BlockSpec((tm, LANE), row),
                   pl.BlockSpec((8, LANE), row)],
        out_shape=[jax.ShapeDtypeStruct((n, D_MODEL), F32),
                   jax.ShapeDtypeStruct((n, LANE), jnp.int32),
                   jax.ShapeDtypeStruct((n, LANE), F32),
                   jax.ShapeDtypeStruct((n // tm * 8, LANE), jnp.int32)],
        compiler_params=_cparams("parallel"),
        name="post_mixer_ml" if ml else "post_mixer_na",
    )(*ins)


def _start_segment_copies(tile, dst_ref, cnt_ref, loff_ref, make_copy):
    def seg(e, carry):
        idx = tile * N_EXPERTS + e
        n8 = cnt_ref[idx]
        l0 = loff_ref[idx]
        g0 = dst_ref[idx]

        def piece(b):
            size = SEG_ALIGN << b
            done = n8 & (size - 1)

            @pl.when((n8 & size) != 0)
            def _():
                make_copy(pl.multiple_of(l0 + done, SEG_ALIGN), pl.multiple_of(g0 + done, SEG_ALIGN), size).start()

        for b in range(SEG_SMALL_BITS):
            piece(b)

        @pl.when(n8 >= (SEG_ALIGN << SEG_SMALL_BITS))
        def _():
            for b in range(SEG_SMALL_BITS, SEG_BITS):
                piece(b)
        return carry

    lax.fori_loop(0, N_EXPERTS, seg, 0)


def _wait_rows(total, make_desc):
    for b in range((LBUF // SEG_ALIGN).bit_length()):
        size = SEG_ALIGN << b

        @pl.when((total & size) != 0)
        def _():
            make_desc(size).wait()


def _pack_bf16_pairs(x, rounded):
    if not rounded:
        x = x.astype(BF16).astype(F32)
    lo = lax.bitcast_convert_type(x[:, :PACKED_W], jnp.uint32) >> 16
    hi = lax.bitcast_convert_type(x[:, PACKED_W:], jnp.uint32) & jnp.uint32(0xFFFF0000)
    return lo | hi


def _unpack_bf16_pairs(u):
    lo = lax.bitcast_convert_type(u << 16, F32)
    hi = lax.bitcast_convert_type(u & jnp.uint32(0xFFFF0000), F32)
    return jnp.concatenate([lo, hi], axis=-1).astype(BF16)


def _local_positions(ri, lrow):
    lane = lax.broadcasted_iota(jnp.int32, ri.shape, 1)
    pos = []
    for k in range(TOP_K):
        off = jnp.sum(jnp.where(lane == ri[:, k:k + 1], lrow, 0).astype(F32), axis=-1, keepdims=True)
        pos.append(off.astype(jnp.int32) + ri[:, TOP_K + k:TOP_K + k + 1])
    return pos


def _dispatch_kernel(dst_ref, cnt_ref, loff_ref, tot_ref, zdst_ref, zcnt_ref, nused_ref,
                     x_ref, ri_ref, rf_ref, lrow_ref, xs_ref, sbuf, zbuf, sems):
    t = pl.program_id(0)
    cur = lax.rem(t, 2)
    tm = x_ref.shape[0]
    p0, p1 = _local_positions(ri_ref[...], lrow_ref[0, 0:1, :])
    lane = lax.broadcasted_iota(jnp.int32, (tm, LANE), 1)
    both = jnp.where(lane == 0, p0, jnp.where(lane == 1, p1, -1)).astype(F32)
    both_t = both.T.astype(jnp.int32)
    row = lax.broadcasted_iota(jnp.int32, (LBUF, tm), 0)
    hits = [row == both_t[k:k + 1, :] for k in range(TOP_K)]
    perm = jnp.where(hits[0] | hits[1], 1.0, 0.0).astype(BF16)
    sbuf[cur, :, 0:PACKED_W] = _pack_bf16_pairs(
        jnp.dot(perm, x_ref[...].astype(BF16), preferred_element_type=F32), rounded=True)
    gates_t = rf_ref[...].T
    gate = jnp.sum(jnp.where(hits[0], gates_t[0:1, :], 0.0) + jnp.where(hits[1], gates_t[1:2, :], 0.0),
                   axis=-1, keepdims=True)
    sbuf[cur, :, PACKED_W:] = lax.bitcast_convert_type(jnp.broadcast_to(gate, (LBUF, LANE)), jnp.uint32)

    def seg_copy(slot):
        def make(lrow, grow, n):
            return pltpu.make_async_copy(sbuf.at[slot, pl.ds(lrow, n)], xs_ref.at[pl.ds(grow, n)], sems.at[slot])
        return make

    _start_segment_copies(t, dst_ref, cnt_ref, loff_ref, seg_copy(cur))

    @pl.when(t > 0)
    def _():
        _wait_rows(tot_ref[t - 1], lambda n: seg_copy(1 - cur)(0, 0, n))

    @pl.when(t == pl.num_programs(0) - 1)
    def _():
        _wait_rows(tot_ref[t], lambda n: seg_copy(cur)(0, 0, n))
        sem = sems.at[0]
        zbuf[...] = jnp.zeros_like(zbuf)
        for wait in (False, True):
            def fill(e, carry):
                n8 = zcnt_ref[e]
                g0 = zdst_ref[e]
                for b in range((MOE_BLK // SEG_ALIGN).bit_length() - 1):
                    size = SEG_ALIGN << b
                    done = n8 & (size - 1)

                    @pl.when((n8 & size) != 0)
                    def _():
                        cp = pltpu.make_async_copy(zbuf.at[pl.ds(0, size)],
                                                   xs_ref.at[pl.ds(pl.multiple_of(g0 + done, SEG_ALIGN), size)], sem)
                        if wait:
                            cp.wait()
                        else:
                            cp.start()
                return carry
            lax.fori_loop(0, N_EXPERTS, fill, 0)

            def fill_tail(h, carry):
                cp = pltpu.make_async_copy(zbuf, xs_ref.at[pl.ds(pl.multiple_of(h * zbuf.shape[0], SEG_ALIGN),
                                                                 zbuf.shape[0])], sem)
                if wait:
                    cp.wait()
                else:
                    cp.start()
                return carry
            halves = MOE_BLK // zbuf.shape[0]
            lax.fori_loop(nused_ref[0] * halves, xs_ref.shape[0] // zbuf.shape[0], fill_tail, 0)


def _dispatch(plan, x1, route_i, route_f, n_rows):
    n = x1.shape[0]
    tm = POST_TM
    row = lambda i, *_: (i, 0)
    return pl.pallas_call(
        _dispatch_kernel,
        grid_spec=pltpu.PrefetchScalarGridSpec(
            num_scalar_prefetch=7,
            grid=(n // tm,),
            in_specs=[pl.BlockSpec((tm, D_MODEL), row),
                      pl.BlockSpec((tm, LANE), row),
                      pl.BlockSpec((tm, LANE), row),
                      pl.BlockSpec((1, 8, LANE), lambda i, *_: (i, 0, 0)),
                      ],
            out_specs=pl.BlockSpec(memory_space=pl.ANY),
            scratch_shapes=[pltpu.VMEM((2, LBUF, XS_W), jnp.uint32),
                            pltpu.VMEM((MOE_BLK // 2, XS_W), jnp.uint32),
                            pltpu.SemaphoreType.DMA((2,))]),
        out_shape=jax.ShapeDtypeStruct((n_rows, XS_W), jnp.uint32),
        compiler_params=_cparams("arbitrary"),
        name="moe_dispatch",
    )(plan["seg_dst"], plan["seg_cnt"], plan["seg_loff"], plan["seg_tot"], plan["fill_dst"], plan["fill_cnt"],
      plan["n_used"], x1, route_i, route_f, plan["loff_rows"])


def _expert_kernel(beid_ref, nused_ref, x_ref, w1_ref, w3_ref, w2_ref, o_ref, w1b, w3b, w2b):
    i = pl.program_id(0)
    prev = beid_ref[jnp.maximum(i - 1, 0)]
    fresh = (i == 0) | (beid_ref[i] != prev)

    @pl.when(fresh)
    def _():
        w1b[...] = w1_ref[...].astype(BF16)
        w3b[...] = w3_ref[...].astype(BF16)
        w2b[...] = w2_ref[...].astype(BF16)

    @pl.when(i < nused_ref[0])
    def _():
        x = _unpack_bf16_pairs(x_ref[:, 0:PACKED_W])
        gate = lax.bitcast_convert_type(x_ref[:, PACKED_W:PACKED_W + 1], F32)
        a = jnp.dot(x, w1b[...], preferred_element_type=F32)
        b = jnp.dot(x, w3b[...], preferred_element_type=F32)
        hmid = (a * jax.nn.sigmoid(a) * b).astype(BF16)
        o_ref[...] = _pack_bf16_pairs(jnp.dot(hmid, w2b[...], preferred_element_type=F32) * gate, rounded=False)

    @pl.when(i >= nused_ref[0])
    def _():
        o_ref[...] = jnp.zeros_like(o_ref)


def _expert_mlp(block_eid, n_used, xs, w1, w3, w2, layer):
    p = xs.shape[0]
    nb = p // MOE_BLK
    wmap13 = lambda i, be, nu: (layer, be[i], 0, 0)
    return pl.pallas_call(
        _expert_kernel,
        grid_spec=pltpu.PrefetchScalarGridSpec(
            num_scalar_prefetch=2,
            grid=(nb,),
            in_specs=[pl.BlockSpec((MOE_BLK, XS_W), lambda i, be, nu: (jnp.maximum(jnp.minimum(i, nu[0] - 1), 0), 0)),
                      pl.BlockSpec((None, None, D_MODEL, D_EXPERT), wmap13),
                      pl.BlockSpec((None, None, D_MODEL, D_EXPERT), wmap13),
                      pl.BlockSpec((None, None, D_EXPERT, D_MODEL), wmap13)],
            out_specs=pl.BlockSpec((MOE_BLK, PACKED_W), lambda i, be, nu: (i, 0)),
            scratch_shapes=[pltpu.VMEM((D_MODEL, D_EXPERT), BF16),
                            pltpu.VMEM((D_MODEL, D_EXPERT), BF16),
                            pltpu.VMEM((D_EXPERT, D_MODEL), BF16)]),
        out_shape=jax.ShapeDtypeStruct((p, PACKED_W), jnp.uint32),
        compiler_params=_cparams("arbitrary"),
        name="moe_experts",
    )(block_eid, n_used, xs, w1, w3, w2)


def _combine_kernel(dst_ref, cnt_ref, loff_ref, tot_ref, x_ref, ri_ref, lrow_ref, g_ref, b_ref, ys_ref,
                    o_ref, ybuf, sems):
    t = pl.program_id(0)
    cur = lax.rem(t, 2)

    def seg_copy(slot):
        def make(lrow, grow, n):
            return pltpu.make_async_copy(ys_ref.at[pl.ds(grow, n)], ybuf.at[slot, pl.ds(lrow, n)], sems.at[slot])
        return make

    @pl.when(t == 0)
    def _():
        ybuf[...] = jnp.zeros_like(ybuf)
        _start_segment_copies(t, dst_ref, cnt_ref, loff_ref, seg_copy(cur))

    @pl.when(t + 1 < pl.num_programs(0))
    def _():
        _start_segment_copies(t + 1, dst_ref, cnt_ref, loff_ref, seg_copy(1 - cur))

    p0, p1 = _local_positions(ri_ref[...], lrow_ref[0, 0:1, :])
    col = lax.broadcasted_iota(jnp.int32, (x_ref.shape[0], LBUF), 1)
    sel = jnp.where((col == p0) | (col == p1), 1.0, 0.0).astype(BF16)
    _wait_rows(tot_ref[t], lambda n: seg_copy(cur)(0, 0, n))

    ffn = jnp.dot(sel, _unpack_bf16_pairs(ybuf[cur]), preferred_element_type=F32)
    o_ref[...] = _layer_norm(ALPHA * x_ref[...] + ffn, g_ref[...], b_ref[...])


def _combine_norm(plan, x1, route_i, ys, ln_g, ln_b):
    n = x1.shape[0]
    tm = POST_TM
    row = lambda i, *_: (i, 0)
    const = lambda i, *_: (0, 0)
    return pl.pallas_call(
        _combine_kernel,
        grid_spec=pltpu.PrefetchScalarGridSpec(
            num_scalar_prefetch=4,
            grid=(n // tm,),
            in_specs=[pl.BlockSpec((tm, D_MODEL), row),
                      pl.BlockSpec((tm, LANE), row),
                      pl.BlockSpec((1, 8, LANE), lambda i, *_: (i, 0, 0)),
                      pl.BlockSpec((1, D_MODEL), const),
                      pl.BlockSpec((1, D_MODEL), const),
                      pl.BlockSpec(memory_space=pl.ANY)],
            out_specs=pl.BlockSpec((tm, D_MODEL), row),
            scratch_shapes=[pltpu.VMEM((2, LBUF, PACKED_W), jnp.uint32),
                            pltpu.SemaphoreType.DMA((2,))]),
        out_shape=jax.ShapeDtypeStruct((n, D_MODEL), F32),
        compiler_params=_cparams("arbitrary"),
        name="moe_combine_norm",
    )(plan["seg_dst"], plan["seg_cnt"], plan["seg_loff"], plan["seg_tot"], x1, route_i,
      plan["loff_rows"], ln_g, ln_b, ys)


def _routing_plan(tile_counts, n_tokens):
    n_tiles = tile_counts.shape[0]
    e_before = jnp.arange(N_EXPERTS)[:, None] < jnp.arange(N_EXPERTS)[None, :]
    t_before = jnp.arange(n_tiles)[:, None] < jnp.arange(n_tiles)[None, :]
    cnt8 = (tile_counts + SEG_ALIGN - 1) // SEG_ALIGN * SEG_ALIGN
    loff = jnp.sum(jnp.where(e_before[None], cnt8[:, :, None], 0), axis=1)
    tot8 = jnp.sum(cnt8, axis=0)
    padded = (tot8 + MOE_BLK - 1) // MOE_BLK * MOE_BLK
    pad_start = jnp.sum(jnp.where(e_before, padded[:, None], 0), axis=0)
    pad_end = pad_start + padded
    seg_dst = pad_start[None, :] + jnp.sum(jnp.where(t_before[:, :, None], cnt8[:, None, :], 0), axis=0)
    worst_rows = n_tokens * TOP_K + n_tiles * N_EXPERTS * (SEG_ALIGN - 1) + N_EXPERTS * (MOE_BLK - 1)
    n_blocks = -(-worst_rows // MOE_BLK)
    block_row = jnp.arange(n_blocks, dtype=jnp.int32) * MOE_BLK
    block_eid = jnp.minimum(jnp.sum(pad_end[None, :] <= block_row[:, None], axis=1), N_EXPERTS - 1).astype(jnp.int32)
    i32 = lambda v: v.astype(jnp.int32)
    loff_rows = jnp.broadcast_to(jnp.pad(i32(loff), ((0, 0), (0, LANE - N_EXPERTS)))[:, None, :],
                                 (n_tiles, 8, LANE))
    return dict(seg_dst=i32(seg_dst).reshape(-1), seg_cnt=i32(cnt8).reshape(-1), seg_loff=i32(loff).reshape(-1),
                seg_tot=i32(jnp.sum(cnt8, axis=1)),
                fill_dst=i32(pad_start + tot8), fill_cnt=i32(padded - tot8), loff_rows=loff_rows,
                block_eid=block_eid, n_used=i32(pad_end[-1] // MOE_BLK).reshape(1), n_rows=n_blocks * MOE_BLK)


def _moe(x1, route_i, route_f, tile_cnt, w1, w3, w2, ln_g, ln_b, layer):
    n = x1.shape[0]
    counts = tile_cnt.reshape(n // POST_TM, 8, LANE)[:, 0, :N_EXPERTS]
    plan = _routing_plan(counts, n)
    xs = _dispatch(plan, x1, route_i, route_f, plan["n_rows"])
    ys = _expert_mlp(plan["block_eid"], plan["n_used"], xs, w1, w3, w2, layer)
    return _combine_norm(plan, x1, route_i, ys, ln_g, ln_b)


def _pad_heads(w):
    lead = w.shape[:-1]
    w = w.reshape(*lead, ML_HEADS, ML_DH)
    w = jnp.pad(w, [(0, 0)] * len(lead) + [(0, 0), (0, ML_DH_PAD - ML_DH)])
    return w.reshape(*lead, ML_W_PAD)


def _ml_in_weight(w):
    tw = TOKEN_W
    segs = [_pad_heads(w[:, i * tw:(i + 1) * tw]) for i in range(4)]
    gates = w[:, 4 * tw:4 * tw + 4 * ML_HEADS]
    qmem = w[:, 4 * tw + 4 * ML_HEADS:]
    gates = jnp.pad(gates, ((0, 0), (0, LANE - 4 * ML_HEADS)))
    return jnp.concatenate(segs + [qmem, gates], axis=1)


def _route_weight(wg, bg, we, be):
    w = jnp.concatenate([wg, we], axis=1)
    b = jnp.concatenate([bg, be], axis=0)
    pad = LANE - w.shape[1]
    w = jnp.pad(w, ((0, 0), (0, pad))).astype(F32)
    w_hi = w.astype(BF16)
    w_lo = (w - w_hi.astype(F32)).astype(BF16)
    return jnp.concatenate([w_hi, w_lo], axis=1), jnp.pad(b, (0, pad)).reshape(1, LANE).astype(F32)


def kernel(x, mem, w_mem_kv, na_w_in, na_rpb, ml_w_in, ml_conv_w, ml_conv_b, ml_gate_b, ml_norm_g, w_out,
           ln1_g, ln1_b, ln2_g, ln2_b, moe_wg, moe_bg, moe_we, moe_be, moe_w1, moe_w3, moe_w2):
    batch, seq, d = x.shape
    n = batch * seq
    rows = seq // GRID_W
    xf = x.reshape(n, d)

    mem_kv = _matmul(mem.reshape(batch * N_MEM, d), w_mem_kv.astype(BF16), BF16,
                     min(PROJ_TM, batch * N_MEM), 2 * MEM_W, "mem_kv").reshape(batch, N_MEM, 2 * MEM_W)

    for layer in range(DEPTH):
        j = layer // 2
        w_route, b_route = _route_weight(moe_wg[layer], moe_bg[layer], moe_we[layer], moe_be[layer])
        g1 = ln1_g[layer].reshape(1, d)
        b1 = ln1_b[layer].reshape(1, d)
        if layer % 2 == 0:
            proj = _matmul(xf, na_w_in[j].astype(BF16), BF16, PROJ_TM, PROJ_TN, "na_in_proj")
            bias_tab, variant_of_group = _na_bias_table(na_rpb[j], rows)
            tok = _neighbourhood_attention(proj.reshape(batch, seq, -1), bias_tab, variant_of_group, batch, seq)
            x1, route_i, route_f, tile_cnt = _post_mixer((tok.reshape(n, TOKEN_W), proj), xf, mem_kv,
                                               w_out[layer].astype(BF16), g1, b1, w_route, b_route, seq, ml=False)
        else:
            proj = _matmul(xf, _ml_in_weight(ml_w_in[j]).astype(BF16), F32, PROJ_TM, PROJ_TN, "ml_in_proj")
            proj3 = proj.reshape(batch, seq, -1)
            conv_w = jnp.concatenate([_pad_heads(ml_conv_w[j][:, :TOKEN_W]),
                                      _pad_heads(ml_conv_w[j][:, TOKEN_W:])], axis=1)
            conv_b = jnp.concatenate([_pad_heads(ml_conv_b[j][:TOKEN_W]),
                                      _pad_heads(ml_conv_b[j][TOKEN_W:])]).reshape(1, 2 * ML_W_PAD)
            col_scale = jnp.concatenate([jnp.full((ML_W_PAD,), ML_DH ** -0.5, F32),
                                         jnp.ones((ML_W_PAD,), F32)]).reshape(1, 2 * ML_W_PAD)
            qk = _conv_silu(proj3, conv_w, conv_b, col_scale, batch, seq)
            gate_b = jnp.pad(ml_gate_b[j].reshape(-1), (0, LANE - 4 * ML_HEADS)).reshape(1, LANE)
            h_f, h_b = _mlstm(qk, proj3, gate_b, batch, seq)
            norm_g = _pad_heads(ml_norm_g[j]).reshape(1, ML_W_PAD)
            wo = jnp.concatenate([_pad_heads(w_out[layer][:TOKEN_W].T).T, w_out[layer][TOKEN_W:]], axis=0)
            x1, route_i, route_f, tile_cnt = _post_mixer((h_f.reshape(n, ML_W_PAD), h_b.reshape(n, ML_W_PAD), proj, norm_g),
                                               xf, mem_kv, wo.astype(BF16), g1, b1, w_route, b_route, seq, ml=True)
        xf = _moe(x1, route_i, route_f, tile_cnt, moe_w1, moe_w3, moe_w2,
                  ln2_g[layer].reshape(1, d), ln2_b[layer].reshape(1, d), layer)
    return xf.reshape(batch, seq, d)
```

```python
import functools
import math

import jax
import jax.numpy as jnp
from jax import lax
from jax.experimental import pallas as pl
from jax.experimental.pallas import tpu as pltpu

D_MODEL = 1024
DEPTH = 4
N_MEM = 256
GRID_W = 64
TOKEN_W = 768
MEM_W = 256
HEAD_DIM = 64
NA_HEADS = 12
WIN_H = 8
WIN_W = 16
ML_HEADS = 4
ML_DH = 192
ML_CHUNK = 256
CONV_K = 5
MEM_HEADS = 4
MEM_DH = 64
N_GROUPS = 4
EXPERTS_PER_GROUP = 8
N_EXPERTS = 32
TOP_K = 2
D_EXPERT = 512
LN_EPS = 1e-5
ALPHA = (2 * DEPTH) ** 0.25

LANE = 128
ML_DH_PAD = 256
ML_W_PAD = ML_HEADS * ML_DH_PAD
VMEM_LIMIT_BYTES = 56 * 1024 * 1024

PROJ_TM = 512
PROJ_TN = 640
NA_GROUP = 4
NA_GROUP_UNROLL = 16
POST_TM = 512
MOE_BLK = 512
ML_INTRA_CHUNKS = 1
ML_INTER_SEQS = 1
SEG_ALIGN = 8
SEG_BITS = (TOP_K * POST_TM // SEG_ALIGN).bit_length()
SEG_SMALL_BITS = 4
PACKED_W = D_MODEL // 2
XS_W = PACKED_W + LANE
LBUF = -(-(TOP_K * POST_TM + N_EXPERTS * (SEG_ALIGN - 1)) // 256) * 256
NEG_BIG = -1e30
LOG2_E = math.log2(math.e)

F32 = jnp.float32
BF16 = jnp.bfloat16


def _cparams(*sem):
    return pltpu.CompilerParams(dimension_semantics=sem, vmem_limit_bytes=VMEM_LIMIT_BYTES)


def _matmul_kernel(x_ref, w_ref, o_ref, *, tn):
    x = x_ref[...].astype(BF16)
    for j in range(w_ref.shape[1] // tn):
        cols = slice(j * tn, (j + 1) * tn)
        o_ref[:, cols] = jnp.dot(x, w_ref[:, cols], preferred_element_type=F32).astype(o_ref.dtype)


def _matmul(x, w, out_dtype, tm, tn, name):
    m, k = x.shape
    n = w.shape[1]
    assert m % tm == 0 and n % tn == 0
    return pl.pallas_call(
        functools.partial(_matmul_kernel, tn=tn),
        grid=(m // tm,),
        in_specs=[pl.BlockSpec((tm, k), lambda i: (i, 0)),
                  pl.BlockSpec((k, n), lambda i: (0, 0))],
        out_specs=pl.BlockSpec((tm, n), lambda i: (i, 0)),
        out_shape=jax.ShapeDtypeStruct((m, n), out_dtype),
        compiler_params=_cparams("parallel"),
        name=name,
    )(x, w)


def _na_bias_table(rpb, rows):
    kh, band, n_groups = _na_geometry(rows)
    g_rows = NA_GROUP
    keys, variant_of_group = [], []
    for g in range(n_groups):
        r0 = g * g_rows
        bs = min(max(r0 - kh // 2, 0), rows - band)
        key = tuple((r0 + i - bs, min(max(r0 + i - kh // 2, 0), rows - kh) - bs) for i in range(g_rows))
        if key not in keys:
            keys.append(key)
        variant_of_group.append(keys.index(key))
    geo = jnp.asarray(keys, jnp.int32)
    qrow, wstart = geo[:, :, 0:1], geo[:, :, 1:2]
    j = jnp.arange(band)[None, None, :]
    row_ok = (j >= wstart) & (j < wstart + kh)
    ri = j - qrow + (WIN_H - 1)
    c = jnp.arange(GRID_W)[:, None]
    kc = jnp.arange(GRID_W)[None, :]
    cs = jnp.clip(c - WIN_W // 2, 0, GRID_W - WIN_W)
    col_ok = (kc >= cs) & (kc < cs + WIN_W)
    ci = kc - c + (WIN_W - 1)
    row_sel = ((ri[..., None] == jnp.arange(2 * WIN_H - 1)) & row_ok[..., None]).astype(F32)
    col_sel = ((ci[:, :, None] == jnp.arange(2 * WIN_W - 1)) & col_ok[..., None]).astype(F32)
    pairs = rpb.astype(F32).reshape(NA_HEADS // 2, 2, 2 * WIN_H - 1, 2 * WIN_W - 1)
    tab = jnp.einsum('phab,vija,ckb->pvhicjk', pairs, row_sel, col_sel, precision=lax.Precision.HIGHEST)
    ok = row_ok[:, :, None, :, None] & col_ok[None, None, :, None, :]
    tab = jnp.where(ok[None, :, None], tab * LOG2_E, NEG_BIG)
    tab = tab.reshape(NA_HEADS // 2, len(keys), 2 * g_rows * GRID_W, band * GRID_W)
    return tab.astype(F32), jnp.asarray(variant_of_group, jnp.int32)


def _na_geometry(rows):
    kh = min(WIN_H, rows)
    band = kh + NA_GROUP - 1
    band = min(band + band % 2, rows)
    assert rows % NA_GROUP == 0
    return kh, band, rows // NA_GROUP


def _na_kernel(var_ref, q_ref, k_ref, v_ref, b_ref, o_ref, *, rows):
    kh, band, n_groups = _na_geometry(rows)
    gq = NA_GROUP * GRID_W
    nk = band * GRID_W
    lane = lax.broadcasted_iota(jnp.int32, (gq, LANE), 1)
    first = lane < HEAD_DIM
    scale = HEAD_DIM ** -0.5

    def body(g, carry):
        r0 = g * NA_GROUP
        bs = jnp.clip(r0 - kh // 2, 0, rows - band)
        q = q_ref[0, pl.ds(pl.multiple_of(r0 * GRID_W, gq), gq), :]
        kb = k_ref[0, pl.ds(pl.multiple_of(bs * GRID_W, GRID_W), nk), :]
        vb = v_ref[0, pl.ds(pl.multiple_of(bs * GRID_W, GRID_W), nk), :]
        zero = jnp.zeros_like(q)
        qq = jnp.concatenate([jnp.where(first, q, zero), jnp.where(first, zero, q)], axis=0)
        s = lax.dot_general(qq, kb, (((1,), (1,)), ((), ())), preferred_element_type=F32)
        s = s * (scale * LOG2_E) + b_ref[0, var_ref[g]]
        m = jnp.max(s, axis=-1, keepdims=True)
        p = jnp.exp2(s - m)
        l = jnp.sum(p, axis=-1, keepdims=True)
        o = jnp.dot(p.astype(BF16), vb, preferred_element_type=F32) / l
        out = jnp.where(first, o[:gq], o[gq:])
        o_ref[0, pl.ds(pl.multiple_of(r0 * GRID_W, gq), gq), :] = out.astype(o_ref.dtype)
        return carry

    lax.fori_loop(0, n_groups, body, 0, unroll=NA_GROUP_UNROLL)


def _neighbourhood_attention(proj, bias_tab, variant_of_group, batch, seq):
    rows = seq // GRID_W
    npair = NA_HEADS // 2
    return pl.pallas_call(
        functools.partial(_na_kernel, rows=rows),
        grid_spec=pltpu.PrefetchScalarGridSpec(
            num_scalar_prefetch=1,
            grid=(batch, npair),
            in_specs=[pl.BlockSpec((1, seq, LANE), lambda b, h, var: (b, 0, h)),
                      pl.BlockSpec((1, seq, LANE), lambda b, h, var: (b, 0, npair + h)),
                      pl.BlockSpec((1, seq, LANE), lambda b, h, var: (b, 0, 2 * npair + h)),
                      pl.BlockSpec((1,) + bias_tab.shape[1:], lambda b, h, var: (h, 0, 0, 0))],
            out_specs=pl.BlockSpec((1, seq, LANE), lambda b, h, var: (b, 0, h))),
        out_shape=jax.ShapeDtypeStruct((batch, seq, TOKEN_W), BF16),
        compiler_params=_cparams("parallel", "arbitrary"),
        name="na_attention",
    )(variant_of_group, proj, proj, proj, bias_tab)


def _conv_silu_kernel(x_ref, w_ref, b_ref, s_ref, o_ref, *, seq):
    x = x_ref[0]
    t = lax.broadcasted_iota(jnp.int32, x.shape, 0)
    acc = jnp.zeros_like(x) + b_ref[...]
    for j in range(CONV_K):
        sh = CONV_K // 2 - j
        if sh == 0:
            xs = x
        elif sh > 0:
            xs = jnp.where(t >= sh, pltpu.roll(x, sh, 0), 0.0)
        else:
            xs = jnp.where(t < seq + sh, pltpu.roll(x, seq + sh, 0), 0.0)
        acc = acc + w_ref[j:j + 1, :] * xs
    y = acc * jax.nn.sigmoid(acc) * s_ref[...]
    o_ref[0] = y.astype(o_ref.dtype)


def _conv_silu(proj, conv_w, conv_b, col_scale, batch, seq):
    ncol = 2 * ML_W_PAD // LANE
    kern = functools.partial(_conv_silu_kernel, seq=seq)
    return pl.pallas_call(
        kern,
        grid=(batch, ncol),
        in_specs=[pl.BlockSpec((1, seq, LANE), lambda b, j: (b, 0, j)),
                  pl.BlockSpec((CONV_K, LANE), lambda b, j: (0, j)),
                  pl.BlockSpec((1, LANE), lambda b, j: (0, j)),
                  pl.BlockSpec((1, LANE), lambda b, j: (0, j))],
        out_specs=pl.BlockSpec((1, seq, LANE), lambda b, j: (b, 0, j)),
        out_shape=jax.ShapeDtypeStruct((batch, seq, 2 * ML_W_PAD), BF16),
        compiler_params=_cparams("parallel", "arbitrary"),
        name="ml_conv_silu",
    )(proj, conv_w, conv_b, col_scale)


def _log_sigmoid(x):
    return jnp.minimum(x, 0.0) - jnp.log1p(jnp.exp(-jnp.abs(x)))


def _mlstm_intra_kernel(q_ref, k_ref, v_ref, g_ref, gb_ref,
                        af_ref, ab_ref, kwf_ref, kwb_ref, vx_ref, scf_ref, scb_ref, *, chunks):
    L = ML_CHUNK
    row = lax.broadcasted_iota(jnp.int32, (L, L), 0)
    col = lax.broadcasted_iota(jnp.int32, (L, L), 1)
    hlane = lax.broadcasted_iota(jnp.int32, (L, ML_DH_PAD), 1)
    glane = lax.broadcasted_iota(jnp.int32, (L, LANE), 1)
    is_f = (glane >= 2 * ML_HEADS) & (glane < 4 * ML_HEADS)
    dirs = ((col <= row, L - 1, af_ref, kwf_ref, scf_ref),
            (col >= row, 0, ab_ref, kwb_ref, scb_ref))

    for c in range(chunks):
        rows = slice(c * L, (c + 1) * L)
        g = g_ref[0, rows, :] + gb_ref[...]
        act = jnp.where(is_f, _log_sigmoid(g), g)
        act_t = act.T
        cums = []
        for valid, _, _, _, _ in dirs:
            bcum = jnp.dot(valid.astype(F32), act, preferred_element_type=F32,
                           precision=lax.Precision.HIGHEST)
            cums.append((bcum, bcum.T))
        scs = [jnp.zeros((L, LANE), F32), jnp.zeros((L, LANE), F32)]

        for h in range(ML_HEADS):
            sl = slice(h * ML_DH_PAD, (h + 1) * ML_DH_PAD)
            qh = q_ref[0, rows, sl]
            kh = k_ref[0, rows, sl]
            vh = jnp.where(hlane == ML_DH, 1.0, v_ref[0, rows, sl]).astype(BF16)
            vx_ref[0, rows, sl] = vh
            s_raw = lax.dot_general(qh, kh, (((1,), (1,)), ((), ())), preferred_element_type=F32)
            kf = kh.astype(F32)
            for d, (valid, last, a_ref, kw_ref, _) in enumerate(dirs):
                bcum, bcum_t = cums[d]
                li_lane = d * ML_HEADS + h
                lf_lane = 2 * ML_HEADS + d * ML_HEADS + h
                b_col = bcum[:, lf_lane:lf_lane + 1]
                b_row = bcum_t[lf_lane:lf_lane + 1, :]
                li_col = act[:, li_lane:li_lane + 1]
                li_row = act_t[li_lane:li_lane + 1, :]
                dm = jnp.where(valid, b_col - b_row + li_row, NEG_BIG)
                mloc = jnp.max(dm, axis=-1, keepdims=True)
                p = (s_raw * jnp.exp(dm - mloc)).astype(BF16)
                a_ref[0, rows, sl] = jnp.dot(p, vh, preferred_element_type=F32)
                b_last = b_col[last:last + 1, :]
                cloc = mloc[last:last + 1, :]
                w_src = jnp.exp(b_last - b_col + li_col - cloc)
                kw_ref[0, rows, sl] = (kf * w_src).astype(BF16)
                scs[d] = jnp.where(glane == h, b_col, jnp.where(glane == ML_HEADS + h, mloc, scs[d]))
        scf_ref[0, rows, :] = scs[0]
        scb_ref[0, rows, :] = scs[1]


def _mlstm_inter_kernel(qf_ref, af_ref, kwf_ref, vxf_ref, scf_ref, qb_ref, ab_ref, kwb_ref, vxb_ref, scb_ref,
                        of_ref, ob_ref, c_ref, m_ref):
    L = ML_CHUNK

    @pl.when(pl.program_id(1) == 0)
    def _():
        c_ref[...] = jnp.zeros_like(c_ref)
        m_ref[...] = jnp.zeros_like(m_ref)

    hlane = lax.broadcasted_iota(jnp.int32, (L, ML_DH_PAD), 1)
    dirs = ((L - 1, qf_ref, af_ref, kwf_ref, vxf_ref, scf_ref, of_ref),
            (0, qb_ref, ab_ref, kwb_ref, vxb_ref, scb_ref, ob_ref))
    for s in range(qf_ref.shape[0]):
        for d, (last, q_ref, a_ref, kw_ref, vx_ref, sc_ref, o_ref) in enumerate(dirs):
            sc = sc_ref[s]
            for h in range(ML_HEADS):
                sl = slice(h * ML_DH_PAD, (h + 1) * ML_DH_PAD)
                b_col = sc[:, h:h + 1]
                mloc = sc[:, ML_HEADS + h:ML_HEADS + h + 1]
                m_prev = m_ref[s, d, h][0:1, 0:1]
                inter = b_col + m_prev
                m_t = jnp.maximum(inter, mloc)
                c_old = c_ref[s, d, h]
                num = (jnp.exp(inter - m_t) * jnp.dot(q_ref[s, :, sl], c_old.astype(BF16),
                                                      preferred_element_type=F32)
                       + jnp.exp(mloc - m_t) * a_ref[s, :, sl])
                den = num[:, ML_DH:ML_DH + 1]
                hout = num / jnp.maximum(jnp.abs(den), jnp.exp(-m_t))
                o_ref[s, :, sl] = jnp.where(hlane < ML_DH, hout, 0.0)

                b_last = b_col[last:last + 1, :]
                cloc = mloc[last:last + 1, :]
                m_new = m_t[last:last + 1, :]
                upd = lax.dot_general(kw_ref[s, :, sl], vx_ref[s, :, sl], (((0,), (0,)), ((), ())),
                                      preferred_element_type=F32)
                c_ref[s, d, h] = jnp.exp(b_last + m_prev - m_new) * c_old + jnp.exp(cloc - m_new) * upd
                m_ref[s, d, h] = jnp.zeros(m_ref.shape[3:], F32) + m_new


def _mlstm(qk, proj, gate_b, batch, seq):
    nc = seq // ML_CHUNK
    chunks = ML_INTRA_CHUNKS
    assert nc % chunks == 0
    tb = chunks * ML_CHUNK
    gcol = (4 * ML_W_PAD + MEM_W) // LANE
    at = lambda col: (lambda b, c: (b, c, col))
    f32w = jax.ShapeDtypeStruct((batch, seq, ML_W_PAD), F32)
    bf16w = jax.ShapeDtypeStruct((batch, seq, ML_W_PAD), BF16)
    scal = jax.ShapeDtypeStruct((batch, seq, LANE), F32)
    wide_p = pl.BlockSpec((1, tb, ML_W_PAD), at(0))
    lane_p = pl.BlockSpec((1, tb, LANE), at(0))
    a_f, a_b, kw_f, kw_b, vx, sc_f, sc_b = pl.pallas_call(
        functools.partial(_mlstm_intra_kernel, chunks=chunks),
        grid=(batch, nc // chunks),
        in_specs=[pl.BlockSpec((1, tb, ML_W_PAD), at(0)),
                  pl.BlockSpec((1, tb, ML_W_PAD), at(1)),
                  pl.BlockSpec((1, tb, ML_W_PAD), at(2)),
                  pl.BlockSpec((1, tb, LANE), at(gcol)),
                  pl.BlockSpec((1, LANE), lambda b, c: (0, 0))],
        out_specs=[wide_p, wide_p, wide_p, wide_p, wide_p, lane_p, lane_p],
        out_shape=[f32w, f32w, bf16w, bf16w, bf16w, scal, scal],
        compiler_params=_cparams("parallel", "parallel"),
        name="mlstm_intra",
    )(qk, qk, proj, proj, gate_b)

    fwd = lambda b, c: (b, c, 0)
    bwd = lambda b, c: (b, nc - 1 - c, 0)
    nseq = ML_INTER_SEQS if batch % ML_INTER_SEQS == 0 else 1
    wide = lambda m: pl.BlockSpec((nseq, ML_CHUNK, ML_W_PAD), m)
    lane = lambda m: pl.BlockSpec((nseq, ML_CHUNK, LANE), m)
    return pl.pallas_call(
        _mlstm_inter_kernel,
        grid=(batch // nseq, nc),
        in_specs=[wide(fwd), wide(fwd), wide(fwd), wide(fwd), lane(fwd),
                  wide(bwd), wide(bwd), wide(bwd), wide(bwd), lane(bwd)],
        out_specs=[wide(fwd), wide(bwd)],
        out_shape=[f32w, f32w],
        scratch_shapes=[pltpu.VMEM((nseq, 2, ML_HEADS, ML_DH_PAD, ML_DH_PAD), F32),
                        pltpu.VMEM((nseq, 2, ML_HEADS, 8, LANE), F32)],
        compiler_params=_cparams("parallel", "arbitrary"),
        name="mlstm_inter",
    )(qk, a_f, kw_f, vx, sc_f, qk, a_b, kw_b, vx, sc_b)


def _layer_norm(y, g, b):
    mu = jnp.mean(y, axis=-1, keepdims=True)
    yc = y - mu
    var = jnp.mean(yc * yc, axis=-1, keepdims=True)
    return yc * lax.rsqrt(var + LN_EPS) * g + b


def _first_argmax(val, lane):
    mx = jnp.max(val, axis=-1, keepdims=True)
    idx = jnp.min(jnp.where(val == mx, lane.astype(F32), float(LANE)), axis=-1, keepdims=True)
    return mx, idx.astype(jnp.int32)


def _route(x1, wr_ref, br_ref):
    x_hi = x1.astype(BF16)
    x_lo = (x1 - x_hi.astype(F32)).astype(BF16)
    t = jnp.dot(x_hi, wr_ref[...], preferred_element_type=F32)
    logits = (t[:, :LANE] + t[:, LANE:] + jnp.dot(x_lo, wr_ref[:, :LANE], preferred_element_type=F32)
              + br_ref[...])
    lane = lax.broadcasted_iota(jnp.int32, logits.shape, 1)
    is_g = lane < N_GROUPS
    gl = jnp.where(is_g, logits, NEG_BIG)
    gmax, gidx = _first_argmax(gl, lane)
    gsum = jnp.sum(jnp.where(is_g, jnp.exp(gl - gmax), 0.0), axis=-1, keepdims=True)
    g_gate = 1.0 / gsum
    lo = N_GROUPS + gidx * EXPERTS_PER_GROUP
    in_grp = (lane >= lo) & (lane < lo + EXPERTS_PER_GROUP)
    el = jnp.where(in_grp, logits, NEG_BIG)
    emax = jnp.max(el, axis=-1, keepdims=True)
    pe = jnp.where(in_grp, jnp.exp(el - emax), 0.0)
    prob = pe / jnp.sum(pe, axis=-1, keepdims=True)
    prob = jnp.where(in_grp, prob, -1.0)
    p1, i1 = _first_argmax(prob, lane)
    p2, i2 = _first_argmax(jnp.where(lane == i1, -1.0, prob), lane)
    tot = p1 + p2
    return i1 - N_GROUPS, i2 - N_GROUPS, g_gate * p1 / tot, g_gate * p2 / tot


def _post_kernel(*refs, ml):
    if ml:
        (hf_ref, hb_ref, op_ref, ng_ref, qm_ref, x_ref, mkv_ref, wo_ref, g_ref, b_ref, wr_ref, br_ref,
         x1_ref, ri_ref, rf_ref, cnt_ref) = refs
        hsum = hf_ref[...] + hb_ref[...]
        hl = lax.broadcasted_iota(jnp.int32, (hsum.shape[0], ML_DH_PAD), 1)
        parts = []
        for h in range(ML_HEADS):
            hh = hsum[:, h * ML_DH_PAD:(h + 1) * ML_DH_PAD]
            mu = jnp.sum(hh, axis=-1, keepdims=True) * (1.0 / ML_DH)
            hc = jnp.where(hl < ML_DH, hh - mu, 0.0)
            var = jnp.sum(hc * hc, axis=-1, keepdims=True) * (1.0 / ML_DH)
            parts.append(hc * lax.rsqrt(var + LN_EPS))
        hn = jnp.concatenate(parts, axis=-1) * ng_ref[...]
        tok = (jax.nn.sigmoid(op_ref[...]) * hn).astype(BF16)
    else:
        (tok_ref, qm_ref, x_ref, mkv_ref, wo_ref, g_ref, b_ref, wr_ref, br_ref,
         x1_ref, ri_ref, rf_ref, cnt_ref) = refs
        tok = tok_ref[...]

    qm = qm_ref[...].astype(BF16)
    mk = mkv_ref[0, :, 0:MEM_W]
    mv = mkv_ref[0, :, MEM_W:2 * MEM_W]
    mlane = lax.broadcasted_iota(jnp.int32, qm.shape, 1)
    matt = jnp.zeros(qm.shape, F32)
    for h in range(MEM_HEADS):
        hm = (mlane >= h * MEM_DH) & (mlane < (h + 1) * MEM_DH)
        qh = jnp.where(hm, qm, jnp.zeros_like(qm))
        s = lax.dot_general(qh, mk, (((1,), (1,)), ((), ())), preferred_element_type=F32) * (MEM_DH ** -0.5)
        p = jnp.exp(s - jnp.max(s, axis=-1, keepdims=True))
        l = jnp.sum(p, axis=-1, keepdims=True)
        o = jnp.dot(p.astype(BF16), mv, preferred_element_type=F32) / l
        matt = jnp.where(hm, o, matt)

    mixed = jnp.concatenate([tok, matt.astype(BF16)], axis=-1)
    y = ALPHA * x_ref[...] + jnp.dot(mixed, wo_ref[...], preferred_element_type=F32)
    x1 = _layer_norm(y, g_ref[...], b_ref[...])
    x1_ref[...] = x1

    e1, e2, g1, g2 = _route(x1, wr_ref, br_ref)
    lane = lax.broadcasted_iota(jnp.int32, ri_ref.shape, 1)
    tm = x1.shape[0]
    oh1 = jnp.where(lane == e1, 1.0, 0.0)
    oh2 = jnp.where(lane == e2, 1.0, 0.0)
    trow = lax.broadcasted_iota(jnp.int32, (tm, tm), 0)
    tcol = lax.broadcasted_iota(jnp.int32, (tm, tm), 1)
    earlier = jnp.where(tcol < trow, 1.0, 0.0).astype(BF16)
    before1 = jnp.dot(earlier, oh1.astype(BF16), preferred_element_type=F32)
    before2 = jnp.dot(earlier, oh2.astype(BF16), preferred_element_type=F32)
    c1 = jnp.sum(oh1, axis=0, keepdims=True)
    c2 = jnp.sum(oh2, axis=0, keepdims=True)
    rank1 = jnp.sum(oh1 * before1, axis=-1, keepdims=True).astype(jnp.int32)
    rank2 = jnp.sum(oh2 * (before2 + c1), axis=-1, keepdims=True).astype(jnp.int32)
    ri_ref[...] = jnp.where(lane == 0, e1, jnp.where(lane == 1, e2,
                            jnp.where(lane == 2, rank1, jnp.where(lane == 3, rank2, 0))))
    rf_ref[...] = jnp.where(lane == 0, g1, jnp.where(lane == 1, g2, 0.0))
    cnt_ref[...] = jnp.broadcast_to(c1 + c2, cnt_ref.shape).astype(jnp.int32)


def _post_mixer(mix_inputs, x, mem_kv, w_out, ln_g, ln_b, w_route, b_route, seq, ml):
    n = x.shape[0]
    tm = POST_TM
    tiles_per_batch = seq // tm
    row = lambda i: (i, 0)
    const = lambda i: (0, 0)
    if ml:
        h_f, h_b, proj, norm_g = mix_inputs
        ins = [h_f, h_b, proj, norm_g, proj]
        specs = [pl.BlockSpec((tm, ML_W_PAD), row),
                 pl.BlockSpec((tm, ML_W_PAD), row),
                 pl.BlockSpec((tm, ML_W_PAD), lambda i: (i, 3)),
                 pl.BlockSpec((1, ML_W_PAD), const),
                 pl.BlockSpec((tm, MEM_W), lambda i: (i, 4 * ML_W_PAD // MEM_W))]
    else:
        tok, proj = mix_inputs
        ins = [tok, proj]
        specs = [pl.BlockSpec((tm, TOKEN_W), row),
                 pl.BlockSpec((tm, MEM_W), lambda i: (i, 3 * TOKEN_W // MEM_W))]
    ins += [x, mem_kv, w_out, ln_g, ln_b, w_route, b_route]
    specs += [pl.BlockSpec((tm, D_MODEL), row),
              pl.BlockSpec((1, N_MEM, 2 * MEM_W), lambda i: (i // tiles_per_batch, 0, 0)),
              pl.BlockSpec(w_out.shape, const),
              pl.BlockSpec((1, D_MODEL), const),
              pl.BlockSpec((1, D_MODEL), const),
              pl.BlockSpec((D_MODEL, 2 * LANE), const),
              pl.BlockSpec((1, LANE), const)]
    return pl.pallas_call(
        functools.partial(_post_kernel, ml=ml),
        grid=(n // tm,),
        in_specs=specs,
        out_specs=[pl.BlockSpec((tm, D_MODEL), row),
                   pl.BlockSpec((tm, LANE), row),
                   pl.BlockSpec((tm, LANE), row),
                   pl.BlockSpec((8, LANE), row)],
        out_shape=[jax.ShapeDtypeStruct((n, D_MODEL), F32),
                   jax.ShapeDtypeStruct((n, LANE), jnp.int32),
                   jax.ShapeDtypeStruct((n, LANE), F32),
                   jax.ShapeDtypeStruct((n // tm * 8, LANE), jnp.int32)],
        compiler_params=_cparams("parallel"),
        name="post_mixer_ml" if ml else "post_mixer_na",
    )(*ins)


def _start_segment_copies(tile, dst_ref, cnt_ref, loff_ref, make_copy):
    def seg(e, carry):
        idx = tile * N_EXPERTS + e
        n8 = cnt_ref[idx]
        l0 = loff_ref[idx]
        g0 = dst_ref[idx]

        def piece(b):
            size = SEG_ALIGN << b
            done = n8 & (size - 1)

            @pl.when((n8 & size) != 0)
            def _():
                make_copy(pl.multiple_of(l0 + done, SEG_ALIGN), pl.multiple_of(g0 + done, SEG_ALIGN), size).start()

        for b in range(SEG_SMALL_BITS):
            piece(b)

        @pl.when(n8 >= (SEG_ALIGN << SEG_SMALL_BITS))
        def _():
            for b in range(SEG_SMALL_BITS, SEG_BITS):
                piece(b)
        return carry

    lax.fori_loop(0, N_EXPERTS, seg, 0)


def _wait_rows(total, make_desc):
    for b in range((LBUF // SEG_ALIGN).bit_length()):
        size = SEG_ALIGN << b

        @pl.when((total & size) != 0)
        def _():
            make_desc(size).wait()


def _pack_bf16_pairs(x, rounded):
    if not rounded:
        x = x.astype(BF16).astype(F32)
    lo = lax.bitcast_convert_type(x[:, :PACKED_W], jnp.uint32) >> 16
    hi = lax.bitcast_convert_type(x[:, PACKED_W:], jnp.uint32) & jnp.uint32(0xFFFF0000)
    return lo | hi


def _unpack_bf16_pairs(u):
    lo = lax.bitcast_convert_type(u << 16, F32)
    hi = lax.bitcast_convert_type(u & jnp.uint32(0xFFFF0000), F32)
    return jnp.concatenate([lo, hi], axis=-1).astype(BF16)


def _local_positions(ri, lrow):
    lane = lax.broadcasted_iota(jnp.int32, ri.shape, 1)
    pos = []
    for k in range(TOP_K):
        off = jnp.sum(jnp.where(lane == ri[:, k:k + 1], lrow, 0).astype(F32), axis=-1, keepdims=True)
        pos.append(off.astype(jnp.int32) + ri[:, TOP_K + k:TOP_K + k + 1])
    return pos


def _dispatch_kernel(dst_ref, cnt_ref, loff_ref, tot_ref, zdst_ref, zcnt_ref, nused_ref,
                     x_ref, ri_ref, rf_ref, lrow_ref, xs_ref, sbuf, zbuf, sems):
    t = pl.program_id(0)
    cur = lax.rem(t, 2)
    tm = x_ref.shape[0]
    p0, p1 = _local_positions(ri_ref[...], lrow_ref[0, 0:1, :])
    lane = lax.broadcasted_iota(jnp.int32, (tm, LANE), 1)
    both = jnp.where(lane == 0, p0, jnp.where(lane == 1, p1, -1)).astype(F32)
    both_t = both.T.astype(jnp.int32)
    row = lax.broadcasted_iota(jnp.int32, (LBUF, tm), 0)
    hits = [row == both_t[k:k + 1, :] for k in range(TOP_K)]
    perm = jnp.where(hits[0] | hits[1], 1.0, 0.0).astype(BF16)
    sbuf[cur, :, 0:PACKED_W] = _pack_bf16_pairs(
        jnp.dot(perm, x_ref[...].astype(BF16), preferred_element_type=F32), rounded=True)
    gates_t = rf_ref[...].T
    gate = jnp.sum(jnp.where(hits[0], gates_t[0:1, :], 0.0) + jnp.where(hits[1], gates_t[1:2, :], 0.0),
                   axis=-1, keepdims=True)
    sbuf[cur, :, PACKED_W:] = lax.bitcast_convert_type(jnp.broadcast_to(gate, (LBUF, LANE)), jnp.uint32)

    def seg_copy(slot):
        def make(lrow, grow, n):
            return pltpu.make_async_copy(sbuf.at[slot, pl.ds(lrow, n)], xs_ref.at[pl.ds(grow, n)], sems.at[slot])
        return make

    _start_segment_copies(t, dst_ref, cnt_ref, loff_ref, seg_copy(cur))

    @pl.when(t > 0)
    def _():
        _wait_rows(tot_ref[t - 1], lambda n: seg_copy(1 - cur)(0, 0, n))

    @pl.when(t == pl.num_programs(0) - 1)
    def _():
        _wait_rows(tot_ref[t], lambda n: seg_copy(cur)(0, 0, n))
        sem = sems.at[0]
        zbuf[...] = jnp.zeros_like(zbuf)
        for wait in (False, True):
            def fill(e, carry):
                n8 = zcnt_ref[e]
                g0 = zdst_ref[e]
                for b in range((MOE_BLK // SEG_ALIGN).bit_length() - 1):
                    size = SEG_ALIGN << b
                    done = n8 & (size - 1)

                    @pl.when((n8 & size) != 0)
                    def _():
                        cp = pltpu.make_async_copy(zbuf.at[pl.ds(0, size)],
                                                   xs_ref.at[pl.ds(pl.multiple_of(g0 + done, SEG_ALIGN), size)], sem)
                        if wait:
                            cp.wait()
                        else:
                            cp.start()
                return carry
            lax.fori_loop(0, N_EXPERTS, fill, 0)

            def fill_tail(h, carry):
                cp = pltpu.make_async_copy(zbuf, xs_ref.at[pl.ds(pl.multiple_of(h * zbuf.shape[0], SEG_ALIGN),
                                                                 zbuf.shape[0])], sem)
                if wait:
                    cp.wait()
                else:
                    cp.start()
                return carry
            halves = MOE_BLK // zbuf.shape[0]
            lax.fori_loop(nused_ref[0] * halves, xs_ref.shape[0] // zbuf.shape[0], fill_tail, 0)


def _dispatch(plan, x1, route_i, route_f, n_rows):
    n = x1.shape[0]
    tm = POST_TM
    row = lambda i, *_: (i, 0)
    return pl.pallas_call(
        _dispatch_kernel,
        grid_spec=pltpu.PrefetchScalarGridSpec(
            num_scalar_prefetch=7,
            grid=(n // tm,),
            in_specs=[pl.BlockSpec((tm, D_MODEL), row),
                      pl.BlockSpec((tm, LANE), row),
                      pl.BlockSpec((tm, LANE), row),
                      pl.BlockSpec((1, 8, LANE), lambda i, *_: (i, 0, 0)),
                      ],
            out_specs=pl.BlockSpec(memory_space=pl.ANY),
            scratch_shapes=[pltpu.VMEM((2, LBUF, XS_W), jnp.uint32),
                            pltpu.VMEM((MOE_BLK // 2, XS_W), jnp.uint32),
                            pltpu.SemaphoreType.DMA((2,))]),
        out_shape=jax.ShapeDtypeStruct((n_rows, XS_W), jnp.uint32),
        compiler_params=_cparams("arbitrary"),
        name="moe_dispatch",
    )(plan["seg_dst"], plan["seg_cnt"], plan["seg_loff"], plan["seg_tot"], plan["fill_dst"], plan["fill_cnt"],
      plan["n_used"], x1, route_i, route_f, plan["loff_rows"])


def _expert_kernel(beid_ref, nused_ref, x_ref, w1_ref, w3_ref, w2_ref, o_ref, w1b, w3b, w2b):
    i = pl.program_id(0)
    prev = beid_ref[jnp.maximum(i - 1, 0)]
    fresh = (i == 0) | (beid_ref[i] != prev)

    @pl.when(fresh)
    def _():
        w1b[...] = w1_ref[...].astype(BF16)
        w3b[...] = w3_ref[...].astype(BF16)
        w2b[...] = w2_ref[...].astype(BF16)

    @pl.when(i < nused_ref[0])
    def _():
        x = _unpack_bf16_pairs(x_ref[:, 0:PACKED_W])
        gate = lax.bitcast_convert_type(x_ref[:, PACKED_W:PACKED_W + 1], F32)
        a = jnp.dot(x, w1b[...], preferred_element_type=F32)
        b = jnp.dot(x, w3b[...], preferred_element_type=F32)
        hmid = (a * jax.nn.sigmoid(a) * b).astype(BF16)
        o_ref[...] = _pack_bf16_pairs(jnp.dot(hmid, w2b[...], preferred_element_type=F32) * gate, rounded=False)

    @pl.when(i >= nused_ref[0])
    def _():
        o_ref[...] = jnp.zeros_like(o_ref)


def _expert_mlp(block_eid, n_used, xs, w1, w3, w2, layer):
    p = xs.shape[0]
    nb = p // MOE_BLK
    wmap13 = lambda i, be, nu: (layer, be[i], 0, 0)
    return pl.pallas_call(
        _expert_kernel,
        grid_spec=pltpu.PrefetchScalarGridSpec(
            num_scalar_prefetch=2,
            grid=(nb,),
            in_specs=[pl.BlockSpec((MOE_BLK, XS_W), lambda i, be, nu: (jnp.maximum(jnp.minimum(i, nu[0] - 1), 0), 0)),
                      pl.BlockSpec((None, None, D_MODEL, D_EXPERT), wmap13),
                      pl.BlockSpec((None, None, D_MODEL, D_EXPERT), wmap13),
                      pl.BlockSpec((None, None, D_EXPERT, D_MODEL), wmap13)],
            out_specs=pl.BlockSpec((MOE_BLK, PACKED_W), lambda i, be, nu: (i, 0)),
            scratch_shapes=[pltpu.VMEM((D_MODEL, D_EXPERT), BF16),
                            pltpu.VMEM((D_MODEL, D_EXPERT), BF16),
                            pltpu.VMEM((D_EXPERT, D_MODEL), BF16)]),
        out_shape=jax.ShapeDtypeStruct((p, PACKED_W), jnp.uint32),
        compiler_params=_cparams("arbitrary"),
        name="moe_experts",
    )(block_eid, n_used, xs, w1, w3, w2)


def _combine_kernel(dst_ref, cnt_ref, loff_ref, tot_ref, x_ref, ri_ref, lrow_ref, g_ref, b_ref, ys_ref,
                    o_ref, ybuf, sems):
    t = pl.program_id(0)
    cur = lax.rem(t, 2)

    def seg_copy(slot):
        def make(lrow, grow, n):
            return pltpu.make_async_copy(ys_ref.at[pl.ds(grow, n)], ybuf.at[slot, pl.ds(lrow, n)], sems.at[slot])
        return make

    @pl.when(t == 0)
    def _():
        ybuf[...] = jnp.zeros_like(ybuf)
        _start_segment_copies(t, dst_ref, cnt_ref, loff_ref, seg_copy(cur))

    @pl.when(t + 1 < pl.num_programs(0))
    def _():
        _start_segment_copies(t + 1, dst_ref, cnt_ref, loff_ref, seg_copy(1 - cur))

    p0, p1 = _local_positions(ri_ref[...], lrow_ref[0, 0:1, :])
    col = lax.broadcasted_iota(jnp.int32, (x_ref.shape[0], LBUF), 1)
    sel = jnp.where((col == p0) | (col == p1), 1.0, 0.0).astype(BF16)
    _wait_rows(tot_ref[t], lambda n: seg_copy(cur)(0, 0, n))

    ffn = jnp.dot(sel, _unpack_bf16_pairs(ybuf[cur]), preferred_element_type=F32)
    o_ref[...] = _layer_norm(ALPHA * x_ref[...] + ffn, g_ref[...], b_ref[...])


def _combine_norm(plan, x1, route_i, ys, ln_g, ln_b):
    n = x1.shape[0]
    tm = POST_TM
    row = lambda i, *_: (i, 0)
    const = lambda i, *_: (0, 0)
    return pl.pallas_call(
        _combine_kernel,
        grid_spec=pltpu.PrefetchScalarGridSpec(
            num_scalar_prefetch=4,
            grid=(n // tm,),
            in_specs=[pl.BlockSpec((tm, D_MODEL), row),
                      pl.BlockSpec((tm, LANE), row),
                      pl.BlockSpec((1, 8, LANE), lambda i, *_: (i, 0, 0)),
                      pl.BlockSpec((1, D_MODEL), const),
                      pl.BlockSpec((1, D_MODEL), const),
                      pl.BlockSpec(memory_space=pl.ANY)],
            out_specs=pl.BlockSpec((tm, D_MODEL), row),
            scratch_shapes=[pltpu.VMEM((2, LBUF, PACKED_W), jnp.uint32),
                            pltpu.SemaphoreType.DMA((2,))]),
        out_shape=jax.ShapeDtypeStruct((n, D_MODEL), F32),
        compiler_params=_cparams("arbitrary"),
        name="moe_combine_norm",
    )(plan["seg_dst"], plan["seg_cnt"], plan["seg_loff"], plan["seg_tot"], x1, route_i,
      plan["loff_rows"], ln_g, ln_b, ys)


def _routing_plan(tile_counts, n_tokens):
    n_tiles = tile_counts.shape[0]
    e_before = jnp.arange(N_EXPERTS)[:, None] < jnp.arange(N_EXPERTS)[None, :]
    t_before = jnp.arange(n_tiles)[:, None] < jnp.arange(n_tiles)[None, :]
    cnt8 = (tile_counts + SEG_ALIGN - 1) // SEG_ALIGN * SEG_ALIGN
    loff = jnp.sum(jnp.where(e_before[None], cnt8[:, :, None], 0), axis=1)
    tot8 = jnp.sum(cnt8, axis=0)
    padded = (tot8 + MOE_BLK - 1) // MOE_BLK * MOE_BLK
    pad_start = jnp.sum(jnp.where(e_before, padded[:, None], 0), axis=0)
    pad_end = pad_start + padded
    seg_dst = pad_start[None, :] + jnp.sum(jnp.where(t_before[:, :, None], cnt8[:, None, :], 0), axis=0)
    worst_rows = n_tokens * TOP_K + n_tiles * N_EXPERTS * (SEG_ALIGN - 1) + N_EXPERTS * (MOE_BLK - 1)
    n_blocks = -(-worst_rows // MOE_BLK)
    block_row = jnp.arange(n_blocks, dtype=jnp.int32) * MOE_BLK
    block_eid = jnp.minimum(jnp.sum(pad_end[None, :] <= block_row[:, None], axis=1), N_EXPERTS - 1).astype(jnp.int32)
    i32 = lambda v: v.astype(jnp.int32)
    loff_rows = jnp.broadcast_to(jnp.pad(i32(loff), ((0, 0), (0, LANE - N_EXPERTS)))[:, None, :],
                                 (n_tiles, 8, LANE))
    return dict(seg_dst=i32(seg_dst).reshape(-1), seg_cnt=i32(cnt8).reshape(-1), seg_loff=i32(loff).reshape(-1),
                seg_tot=i32(jnp.sum(cnt8, axis=1)),
                fill_dst=i32(pad_start + tot8), fill_cnt=i32(padded - tot8), loff_rows=loff_rows,
                block_eid=block_eid, n_used=i32(pad_end[-1] // MOE_BLK).reshape(1), n_rows=n_blocks * MOE_BLK)


def _moe(x1, route_i, route_f, tile_cnt, w1, w3, w2, ln_g, ln_b, layer):
    n = x1.shape[0]
    counts = tile_cnt.reshape(n // POST_TM, 8, LANE)[:, 0, :N_EXPERTS]
    plan = _routing_plan(counts, n)
    xs = _dispatch(plan, x1, route_i, route_f, plan["n_rows"])
    ys = _expert_mlp(plan["block_eid"], plan["n_used"], xs, w1, w3, w2, layer)
    return _combine_norm(plan, x1, route_i, ys, ln_g, ln_b)


def _pad_heads(w):
    lead = w.shape[:-1]
    w = w.reshape(*lead, ML_HEADS, ML_DH)
    w = jnp.pad(w, [(0, 0)] * len(lead) + [(0, 0), (0, ML_DH_PAD - ML_DH)])
    return w.reshape(*lead, ML_W_PAD)


def _ml_in_weight(w):
    tw = TOKEN_W
    segs = [_pad_heads(w[:, i * tw:(i + 1) * tw]) for i in range(4)]
    gates = w[:, 4 * tw:4 * tw + 4 * ML_HEADS]
    qmem = w[:, 4 * tw + 4 * ML_HEADS:]
    gates = jnp.pad(gates, ((0, 0), (0, LANE - 4 * ML_HEADS)))
    return jnp.concatenate(segs + [qmem, gates], axis=1)


def _route_weight(wg, bg, we, be):
    w = jnp.concatenate([wg, we], axis=1)
    b = jnp.concatenate([bg, be], axis=0)
    pad = LANE - w.shape[1]
    w = jnp.pad(w, ((0, 0), (0, pad))).astype(F32)
    w_hi = w.astype(BF16)
    w_lo = (w - w_hi.astype(F32)).astype(BF16)
    return jnp.concatenate([w_hi, w_lo], axis=1), jnp.pad(b, (0, pad)).reshape(1, LANE).astype(F32)


def kernel(x, mem, w_mem_kv, na_w_in, na_rpb, ml_w_in, ml_conv_w, ml_conv_b, ml_gate_b, ml_norm_g, w_out,
           ln1_g, ln1_b, ln2_g, ln2_b, moe_wg, moe_bg, moe_we, moe_be, moe_w1, moe_w3, moe_w2):
    batch, seq, d = x.shape
    n = batch * seq
    rows = seq // GRID_W
    xf = x.reshape(n, d)

    mem_kv = _matmul(mem.reshape(batch * N_MEM, d), w_mem_kv.astype(BF16), BF16,
                     min(PROJ_TM, batch * N_MEM), 2 * MEM_W, "mem_kv").reshape(batch, N_MEM, 2 * MEM_W)

    for layer in range(DEPTH):
        j = layer // 2
        w_route, b_route = _route_weight(moe_wg[layer], moe_bg[layer], moe_we[layer], moe_be[layer])
        g1 = ln1_g[layer].reshape(1, d)
        b1 = ln1_b[layer].reshape(1, d)
        if layer % 2 == 0:
            proj = _matmul(xf, na_w_in[j].astype(BF16), BF16, PROJ_TM, PROJ_TN, "na_in_proj")
            bias_tab, variant_of_group = _na_bias_table(na_rpb[j], rows)
            tok = _neighbourhood_attention(proj.reshape(batch, seq, -1), bias_tab, variant_of_group, batch, seq)
            x1, route_i, route_f, tile_cnt = _post_mixer((tok.reshape(n, TOKEN_W), proj), xf, mem_kv,
                                               w_out[layer].astype(BF16), g1, b1, w_route, b_route, seq, ml=False)
        else:
            proj = _matmul(xf, _ml_in_weight(ml_w_in[j]).astype(BF16), F32, PROJ_TM, PROJ_TN, "ml_in_proj")
            proj3 = proj.reshape(batch, seq, -1)
            conv_w = jnp.concatenate([_pad_heads(ml_conv_w[j][:, :TOKEN_W]),
                                      _pad_heads(ml_conv_w[j][:, TOKEN_W:])], axis=1)
            conv_b = jnp.concatenate([_pad_heads(ml_conv_b[j][:TOKEN_W]),
                                      _pad_heads(ml_conv_b[j][TOKEN_W:])]).reshape(1, 2 * ML_W_PAD)
            col_scale = jnp.concatenate([jnp.full((ML_W_PAD,), ML_DH ** -0.5, F32),
                                         jnp.ones((ML_W_PAD,), F32)]).reshape(1, 2 * ML_W_PAD)
            qk = _conv_silu(proj3, conv_w, conv_b, col_scale, batch, seq)
            gate_b = jnp.pad(ml_gate_b[j].reshape(-1), (0, LANE - 4 * ML_HEADS)).reshape(1, LANE)
            h_f, h_b = _mlstm(qk, proj3, gate_b, batch, seq)
            norm_g = _pad_heads(ml_norm_g[j]).reshape(1, ML_W_PAD)
            wo = jnp.concatenate([_pad_heads(w_out[layer][:TOKEN_W].T).T, w_out[layer][TOKEN_W:]], axis=0)
            x1, route_i, route_f, tile_cnt = _post_mixer((h_f.reshape(n, ML_W_PAD), h_b.reshape(n, ML_W_PAD), proj, norm_g),
                                               xf, mem_kv, wo.astype(BF16), g1, b1, w_route, b_route, seq, ml=True)
        xf = _moe(x1, route_i, route_f, tile_cnt, moe_w1, moe_w3, moe_w2,
                  ln2_g[layer].reshape(1, d), ln2_b[layer].reshape(1, d), layer)
    return xf.reshape(batch, seq, d)
```
